```python
import jax, jax.numpy as jnp
from jax import lax
import numpy as np

D_MODEL = 1024
BATCH = 4
SEQ = 8192
DEPTH = 1

HEAD_DIM = 64
A_HEADS = 8
A_KV = 2
WINDOW = 128
BLOCK = 128
B_HEADS = 8
IDX_HEADS = 8
IDX_DIM = 64
TOPK_MAX = 256
D_FF = 2816
D_PLE = 256
EPS = 1e-6
NEG = -1e30

A_Q = A_HEADS * HEAD_DIM
A_KVW = A_KV * HEAD_DIM
B_Q = B_HEADS * HEAD_DIM
B_KVW = HEAD_DIM
I_Q = IDX_HEADS * IDX_DIM
I_K = IDX_DIM
I_W = IDX_HEADS
D_IN = A_Q + 2 * A_KVW + B_Q + 2 * B_KVW + I_Q + I_K + I_W + 2 * D_MODEL

kernel_name = "hybrid_swa_sink_dsa_macaron_ple"


def _split_points():
    sizes = [A_Q, A_KVW, A_KVW, B_Q, B_KVW, B_KVW, I_Q, I_K, I_W, D_MODEL, D_MODEL]
    pts, acc = [], 0
    for s in sizes[:-1]:
        acc += s
        pts.append(acc)
    return pts


def rms_norm(x, g):
    xf = x.astype(jnp.float32)
    y = xf * lax.rsqrt(jnp.mean(xf * xf, axis=-1, keepdims=True) + EPS)
    return (y * g.astype(jnp.float32)).astype(x.dtype)


def swiglu(x, w_in, w_out):
    a, b = jnp.split(x @ w_in, 2, axis=-1)
    return (jax.nn.silu(a) * b) @ w_out


def alibi_slopes():
    n = A_HEADS + B_HEADS
    return 2.0 ** (-8.0 * jnp.arange(1, n + 1, dtype=jnp.float32) / n)


def sliding_window_sink_attention(q, k, v, sink, slopes):
    bsz, t = q.shape[:2]
    nb = t // BLOCK
    r = A_HEADS // A_KV
    qb = q.reshape(bsz, nb, BLOCK, A_KV, r, HEAD_DIM)
    kb = k.reshape(bsz, nb, BLOCK, A_KV, HEAD_DIM)
    vb = v.reshape(bsz, nb, BLOCK, A_KV, HEAD_DIM)
    pad = ((0, 0), (1, 0), (0, 0), (0, 0), (0, 0))
    kc = jnp.concatenate([jnp.pad(kb, pad)[:, :-1], kb], axis=2)
    vc = jnp.concatenate([jnp.pad(vb, pad)[:, :-1], vb], axis=2)
    logits = jnp.einsum('bnqgrd,bnkgd->bngrqk', qb, kc).astype(jnp.float32) * (HEAD_DIM ** -0.5)
    qi = jnp.arange(BLOCK)
    kj = jnp.arange(2 * BLOCK)
    dist = BLOCK + qi[:, None] - kj[None, :]
    s_pos = (jnp.arange(nb)[:, None] - 1) * BLOCK + kj[None, :]
    valid = ((dist >= 0) & (dist < WINDOW))[None] & (s_pos >= 0)[:, None, :]
    sl = slopes.reshape(A_KV, r)[None, None, :, :, None, None]
    logits = logits - sl * dist.astype(jnp.float32)
    logits = jnp.where(valid[None, :, None, None], logits, NEG)
    sink_f = sink.astype(jnp.float32).reshape(A_KV, r)[None, None, :, :, None, None]
    m = jnp.maximum(jnp.max(logits, axis=-1, keepdims=True), sink_f)
    e = jnp.exp(logits - m)
    probs = e / (jnp.sum(e, axis=-1, keepdims=True) + jnp.exp(sink_f - m))
    out = jnp.einsum('bngrqk,bnkgd->bnqgrd', probs.astype(v.dtype), vc)
    return out.reshape(bsz, t, A_HEADS * HEAD_DIM)


def indexed_sparse_attention(q, k, v, qi, ki, wi, slopes):
    bsz, t = q.shape[:2]
    nb = t // BLOCK
    topk = min(TOPK_MAX, t // 4)
    key_pos = jnp.arange(t)
    gather = jax.vmap(lambda tab, idx: tab[idx])

    def to_blocks(a):
        return jnp.moveaxis(a.reshape((bsz, nb, BLOCK) + a.shape[2:]), 1, 0)

    def block(args):
        n, qb, qib, wib = args
        tq = n * BLOCK + jnp.arange(BLOCK)
        rel = jax.nn.relu(jnp.einsum('bqhd,bsd->bqhs', qib, ki).astype(jnp.float32) * (IDX_DIM ** -0.5))
        score = jnp.einsum('bqhs,bqh->bqs', rel, wib.astype(jnp.float32) * (IDX_HEADS ** -0.5))
        causal = key_pos[None, :] <= tq[:, None]
        score = jnp.where(causal[None], score, NEG)
        _, idx = lax.top_k(score, topk)
        kg = gather(k, idx)
        vg = gather(v, idx)
        dist = tq[None, :, None] - idx
        logits = jnp.einsum('bqhd,bqkd->bqhk', qb, kg).astype(jnp.float32) * (HEAD_DIM ** -0.5)
        logits = logits - slopes[None, None, :, None] * dist[:, :, None, :].astype(jnp.float32)
        logits = jnp.where((dist >= 0)[:, :, None, :], logits, NEG)
        probs = jax.nn.softmax(logits, axis=-1)
        return jnp.einsum('bqhk,bqkd->bqhd', probs.astype(v.dtype), vg)

    out = lax.map(block, (jnp.arange(nb), to_blocks(q), to_blocks(qi), to_blocks(wi)))
    return jnp.moveaxis(out, 0, 1).reshape(bsz, t, B_HEADS * HEAD_DIM)


def setup_inputs(seed: int = 0) -> dict:
    key = jax.random.key(seed)
    ks = jax.random.split(key, 20)
    f32 = jnp.float32

    def w(k, shape, fan_in):
        return jax.random.normal(k, shape, f32) * (fan_in ** -0.5)

    def gain(k, shape):
        return 1.0 + 0.05 * jax.random.normal(k, shape, f32)

    return {
        "x": jax.random.normal(ks[0], (BATCH, SEQ, D_MODEL), f32),
        "p": jax.random.normal(ks[1], (DEPTH, BATCH, SEQ, D_PLE), f32),
        "ln_ffn1": gain(ks[2], (DEPTH, D_MODEL)),
        "w_ffn1_in": w(ks[3], (DEPTH, D_MODEL, 2 * D_FF), D_MODEL),
        "w_ffn1_out": w(ks[4], (DEPTH, D_FF, D_MODEL), D_FF),
        "ln_mix": gain(ks[5], (DEPTH, D_MODEL)),
        "w_in": w(ks[6], (DEPTH, D_MODEL, D_IN), D_MODEL),
        "a_sink": 0.5 * jax.random.normal(ks[7], (DEPTH, A_HEADS), f32),
        "w_br_a": w(ks[8], (DEPTH, A_Q, D_MODEL), A_Q),
        "w_br_b": w(ks[9], (DEPTH, B_Q, D_MODEL), B_Q),
        "w_out": w(ks[10], (DEPTH, D_MODEL, D_MODEL), D_MODEL),
        "ln_ffn2": gain(ks[11], (DEPTH, D_MODEL)),
        "w_ffn2_in": w(ks[12], (DEPTH, D_MODEL, 2 * D_FF), D_MODEL),
        "w_ffn2_out": w(ks[13], (DEPTH, D_FF, D_MODEL), D_FF),
        "ln_ple": gain(ks[14], (DEPTH, D_MODEL)),
        "w_ple_gate": w(ks[15], (DEPTH, D_MODEL, D_MODEL), D_MODEL),
        "w_ple_proj": w(ks[16], (DEPTH, D_PLE, D_MODEL), D_PLE),
        "ln_final": gain(ks[17], (D_MODEL,)),
    }


def reference(x, p, ln_ffn1, w_ffn1_in, w_ffn1_out, ln_mix, w_in, a_sink, w_br_a, w_br_b,
              w_out, ln_ffn2, w_ffn2_in, w_ffn2_out, ln_ple, w_ple_gate, w_ple_proj, ln_final):
    bsz, t, _ = x.shape
    slopes = alibi_slopes()
    slopes_a = slopes[:A_HEADS].astype(jnp.float32)
    slopes_b = slopes[A_HEADS:].astype(jnp.float32)
    split_pts = _split_points()
    h = x
    for i in range(DEPTH):
        h = h + 0.5 * swiglu(rms_norm(h, ln_ffn1[i]), w_ffn1_in[i], w_ffn1_out[i])
        u = rms_norm(h, ln_mix[i])
        aq, ak, av, bq, bk, bv, iq, ik, iw, ga, gb = jnp.split(u @ w_in[i], split_pts, axis=-1)
        ya = sliding_window_sink_attention(
            aq.reshape(bsz, t, A_HEADS, HEAD_DIM),
            ak.reshape(bsz, t, A_KV, HEAD_DIM),
            av.reshape(bsz, t, A_KV, HEAD_DIM),
            a_sink[i], slopes_a)
        yb = indexed_sparse_attention(
            bq.reshape(bsz, t, B_HEADS, HEAD_DIM), bk, bv,
            iq.reshape(bsz, t, IDX_HEADS, IDX_DIM), ik, iw, slopes_b)
        mix = jax.nn.sigmoid(ga) * (ya @ w_br_a[i]) + jax.nn.sigmoid(gb) * (yb @ w_br_b[i])
        h = h + mix @ w_out[i]
        h = h + 0.5 * swiglu(rms_norm(h, ln_ffn2[i]), w_ffn2_in[i], w_ffn2_out[i])
        gate = jax.nn.sigmoid(rms_norm(h, ln_ple[i]) @ w_ple_gate[i])
        h = h + gate * (p[i] @ w_ple_proj[i])
    return rms_norm(h, ln_final)
```

```python
import functools

import numpy as np
import jax
import jax.numpy as jnp
from jax import lax
from jax.experimental import pallas as pl
from jax.experimental.pallas import tpu as pltpu

F32 = jnp.float32
BF16 = jnp.bfloat16
I32 = jnp.int32

D_MODEL = 1024
HEAD_DIM = 64
A_HEADS = 8
A_KV = 2
BLOCK = 128
B_HEADS = 8
IDX_HEADS = 8
TOPK = 256
D_FF = 2816
D_PLE = 256
EPS = 1e-6
NEG = -1e30
INT_MIN = -(2 ** 31)

LANES = 128
VMEM_LIMIT = 48 * 1024 * 1024

_SLOPES = [float(np.float32(2.0) ** np.float32(-8.0 * i / (A_HEADS + B_HEADS)))
           for i in range(1, A_HEADS + B_HEADS + 1)]
SLOPES_A = _SLOPES[:A_HEADS]
SLOPES_B = _SLOPES[A_HEADS:]

NT = (((1,), (1,)), ((), ()))


def _rms(x, g):
    return x * lax.rsqrt(jnp.mean(x * x, axis=-1, keepdims=True) + EPS) * g


def _cparams(*sem):
    return pltpu.CompilerParams(dimension_semantics=sem, vmem_limit_bytes=VMEM_LIMIT)


FFN_TM = 512
FFN_TF = 1408


def _ffn_kernel(h_ref, ln_ref, wa_ref, wb_ref, wo_ref, o_ref, xn_ref, acc_ref):
    j = pl.program_id(1)

    @pl.when(j == 0)
    def _():
        xn_ref[...] = _rms(h_ref[...], ln_ref[...]).astype(BF16)
        acc_ref[...] = jnp.zeros_like(acc_ref)

    xn = xn_ref[...]
    a = jnp.dot(xn, wa_ref[...], preferred_element_type=F32)
    b = jnp.dot(xn, wb_ref[...], preferred_element_type=F32)
    g = (a * jax.nn.sigmoid(a) * b).astype(BF16)
    acc_ref[...] += jnp.dot(g, wo_ref[...], preferred_element_type=F32)

    @pl.when(j == pl.num_programs(1) - 1)
    def _():
        o_ref[...] = h_ref[...] + 0.5 * acc_ref[...]


def _ffn(h, ln, w_in, w_out):
    n = h.shape[0]
    nj = D_FF // FFN_TF
    return pl.pallas_call(
        _ffn_kernel,
        grid=(n // FFN_TM, nj),
        in_specs=[
            pl.BlockSpec((FFN_TM, D_MODEL), lambda i, j: (i, 0)),
            pl.BlockSpec((1, D_MODEL), lambda i, j: (0, 0)),
            pl.BlockSpec((D_MODEL, FFN_TF), lambda i, j: (0, j)),
            pl.BlockSpec((D_MODEL, FFN_TF), lambda i, j: (0, j + nj)),
            pl.BlockSpec((FFN_TF, D_MODEL), lambda i, j: (j, 0)),
        ],
        out_specs=pl.BlockSpec((FFN_TM, D_MODEL), lambda i, j: (i, 0)),
        out_shape=jax.ShapeDtypeStruct((n, D_MODEL), F32),
        scratch_shapes=[pltpu.VMEM((FFN_TM, D_MODEL), BF16), pltpu.VMEM((FFN_TM, D_MODEL), F32)],
        compiler_params=_cparams("parallel", "arbitrary"),
        name="ffn",
    )(h, ln, w_in, w_in, w_out)


PROJ_TM = 256


def _inproj_kernel(h_ref, ln_ref, waq, wakv, wiq, wbq, wkall, wiw, wg,
                   o_aq, o_akv, o_iq, o_bq, o_kall, o_iw, o_g):
    u = _rms(h_ref[...], ln_ref[...]).astype(BF16)

    def mm(w):
        return jnp.dot(u, w[...], preferred_element_type=F32)

    o_aq[...] = mm(waq).astype(BF16)
    o_akv[...] = mm(wakv).astype(BF16)
    o_iq[...] = mm(wiq).astype(BF16)
    o_bq[...] = mm(wbq).astype(BF16)
    o_kall[...] = mm(wkall).astype(BF16)
    o_iw[...] = mm(wiw)
    o_g[...] = jax.nn.sigmoid(mm(wg))


def _pad_heads(w, left):
    d = w.shape[0]
    w3 = w.reshape(d, -1, HEAD_DIM)
    pad = ((0, 0), (0, 0), (HEAD_DIM, 0)) if left else ((0, 0), (0, 0), (0, HEAD_DIM))
    return jnp.pad(w3, pad).reshape(d, -1)


def _split_in_proj(w):
    scale = HEAD_DIM ** -0.5
    aq, ak, av = w[:, 0:512] * scale, w[:, 512:640], w[:, 640:768]
    bq, bk, bv = w[:, 768:1280] * scale, w[:, 1280:1344], w[:, 1344:1408]
    iq, ik, iw = w[:, 1408:1920] * scale, w[:, 1920:1984], w[:, 1984:1992]
    g = w[:, 1992:4040]
    z64 = jnp.zeros((w.shape[0], HEAD_DIM), w.dtype)
    return dict(
        waq=aq.astype(BF16),
        wakv=jnp.concatenate([ak, av], axis=1).astype(BF16),
        wiq=_pad_heads(iq, left=False).astype(BF16),
        wbq=_pad_heads(bq, left=True).astype(BF16),
        wkall=jnp.concatenate([ik, bk, bv, z64], axis=1).astype(BF16),
        wiw=jnp.pad(iw, ((0, 0), (0, LANES - IDX_HEADS))).astype(BF16),
        wg=g.astype(BF16),
    )


def _in_proj(h, ln, ws):
    n = h.shape[0]
    names = ["waq", "wakv", "wiq", "wbq", "wkall", "wiw", "wg"]
    widths = [ws[k].shape[1] for k in names]
    dtypes = [BF16, BF16, BF16, BF16, BF16, F32, F32]
    return pl.pallas_call(
        _inproj_kernel,
        grid=(n // PROJ_TM,),
        in_specs=[pl.BlockSpec((PROJ_TM, D_MODEL), lambda i: (i, 0)),
                  pl.BlockSpec((1, D_MODEL), lambda i: (0, 0))]
        + [pl.BlockSpec((D_MODEL, wd), lambda i: (0, 0)) for wd in widths],
        out_specs=[pl.BlockSpec((PROJ_TM, wd), lambda i: (i, 0)) for wd in widths],
        out_shape=[jax.ShapeDtypeStruct((n, wd), dt) for wd, dt in zip(widths, dtypes)],
        compiler_params=_cparams("parallel"),
        name="in_proj",
    )(h, ln, *[ws[k] for k in names])


def _swa_kernel(sink_ref, q_ref, kvp_ref, kvc_ref, o_ref):
    n = pl.program_id(1)
    q = q_ref[0]
    kvp = kvp_ref[0]
    kvc = kvc_ref[0]
    i = lax.broadcasted_iota(I32, (BLOCK, BLOCK), 0)
    j = lax.broadcasted_iota(I32, (BLOCK, BLOCK), 1)
    lower = j <= i
    dcur = (i - j).astype(F32)
    dprev = (i - j + BLOCK).astype(F32)
    prev_bias = jnp.where(n > 0, 0.0, NEG).astype(F32)
    outs = []
    for h in range(A_HEADS):
        g = h // (A_HEADS // A_KV)
        qh = q[:, h * HEAD_DIM:(h + 1) * HEAD_DIM]
        kp = kvp[:, g * HEAD_DIM:(g + 1) * HEAD_DIM]
        kc = kvc[:, g * HEAD_DIM:(g + 1) * HEAD_DIM]
        vp = kvp[:, BLOCK + g * HEAD_DIM:BLOCK + (g + 1) * HEAD_DIM]
        vc = kvc[:, BLOCK + g * HEAD_DIM:BLOCK + (g + 1) * HEAD_DIM]
        sp = lax.dot_general(qh, kp, NT, preferred_element_type=F32)
        sc = lax.dot_general(qh, kc, NT, preferred_element_type=F32)
        s = jnp.where(lower, sc - SLOPES_A[h] * dcur, sp - SLOPES_A[h] * dprev + prev_bias)
        sink = sink_ref[h]
        m = jnp.maximum(jnp.max(s, axis=1, keepdims=True), sink)
        e = jnp.exp(s - m)
        p = e / (jnp.sum(e, axis=1, keepdims=True) + jnp.exp(sink - m))
        pc = jnp.where(lower, p, 0.0).astype(BF16)
        pp = jnp.where(lower, 0.0, p).astype(BF16)
        outs.append(jnp.dot(pc, vc, preferred_element_type=F32)
                    + jnp.dot(pp, vp, preferred_element_type=F32))
    o_ref[0] = jnp.concatenate(outs, axis=1).astype(BF16)


def _swa(sink, aq, akv):
    bsz, t, _ = aq.shape
    nb = t // BLOCK
    return pl.pallas_call(
        _swa_kernel,
        grid=(bsz, nb),
        in_specs=[
            pl.BlockSpec(memory_space=pltpu.SMEM),
            pl.BlockSpec((1, BLOCK, A_HEADS * HEAD_DIM), lambda b, n: (b, n, 0)),
            pl.BlockSpec((1, BLOCK, 2 * A_KV * HEAD_DIM), lambda b, n: (b, jnp.maximum(n - 1, 0), 0)),
            pl.BlockSpec((1, BLOCK, 2 * A_KV * HEAD_DIM), lambda b, n: (b, n, 0)),
        ],
        out_specs=pl.BlockSpec((1, BLOCK, A_HEADS * HEAD_DIM), lambda b, n: (b, n, 0)),
        out_shape=jax.ShapeDtypeStruct((bsz, t, A_HEADS * HEAD_DIM), BF16),
        compiler_params=_cparams("parallel", "arbitrary"),
        name="swa",
    )(sink, aq, akv, akv)


CK = 256


def _sortable(x):
    bits = pltpu.bitcast(x, I32)
    return jnp.where(bits < 0, -(bits & 0x7FFFFFFF), bits)


def _dsa_kernel(iq_ref, iw_ref, bq_ref, kall_ref, yb_ref, keys_ref, thr_ref, m_ref, acc_ref):
    n = pl.program_id(1)
    nck = n // 2 + 1
    tq = n * BLOCK + lax.broadcasted_iota(I32, (CK, LANES), 1)
    rel = lax.broadcasted_iota(I32, (CK, LANES), 0)

    def stack_heads(x):
        return jnp.concatenate([x[:, h * LANES:(h + 1) * LANES] for h in range(B_HEADS)], axis=0)

    qi = stack_heads(iq_ref[0])
    iw_t = (iw_ref[0] * (IDX_HEADS ** -0.5)).T

    def score_chunk(c, smax):
        start = pl.multiple_of(c * CK, CK)
        kc = kall_ref[0, pl.ds(start, CK), 0:LANES]
        s = lax.dot_general(kc, qi, NT, preferred_element_type=F32)
        acc = jnp.zeros((CK, LANES), F32)
        for h in range(IDX_HEADS):
            acc = acc + jnp.maximum(s[:, h * LANES:(h + 1) * LANES], 0.0) * iw_t[h:h + 1, :]
        sc = jnp.where(c * CK + rel <= tq, acc, NEG)
        keys_ref[c] = _sortable(sc)
        return jnp.maximum(smax, jnp.max(sc, axis=0, keepdims=True))

    smax = lax.fori_loop(0, nck, score_chunk, jnp.full((1, LANES), NEG, F32))

    thr_ref[...] = jnp.full(thr_ref.shape, INT_MIN, I32)

    @pl.when(n >= 2)
    def _():
        def count_ge(probe):
            def body(c, cnt):
                hit = jnp.where(keys_ref[c] >= probe, 1.0, 0.0)
                return cnt + jnp.sum(hit.reshape(CK // 8, 8, LANES), axis=0)
            cnt = lax.fori_loop(0, nck, body, jnp.zeros((8, LANES), F32))
            return jnp.sum(cnt, axis=0, keepdims=True)

        def done(lo, hi, c_lo):
            return (c_lo == TOPK) | (hi - lo == 1)

        def cond(st):
            lo, hi, c_lo, c_hi, found, lvl = st
            return jnp.max(jnp.where(done(lo, hi, c_lo), 0.0, 1.0)) > 0.0

        def step(st):
            lo, hi, c_lo, c_hi, found, lvl = st
            mid = lo + lax.shift_right_logical(hi - lo, 1)
            jump = lax.shift_left(jnp.ones_like(lvl), 22 + 2 * jnp.minimum(lvl, 4))
            reach = (lvl <= 4) & (hi >= INT_MIN + jump)
            gallop = jnp.where(reach, hi - jump, INT_MIN)
            probe = jnp.where(found > 0, mid, gallop)
            c = count_ge(probe)
            ge = c >= TOPK
            return (jnp.where(ge, probe, lo), jnp.where(ge, hi, probe),
                    jnp.where(ge, c, c_lo), jnp.where(ge, c_hi, c),
                    jnp.where(ge, 1, found), lvl + 1)

        width = (nck * CK).astype(F32)
        st0 = (jnp.full((1, LANES), INT_MIN, I32), _sortable(smax) + 1,
               jnp.zeros((1, LANES), F32) + width, jnp.zeros((1, LANES), F32),
               jnp.zeros((1, LANES), I32), jnp.zeros((1, LANES), I32))
        lo, hi, c_lo, c_hi, _, _ = lax.while_loop(cond, step, st0)
        thr_ref[...] = jnp.broadcast_to(lo, thr_ref.shape)

        tied = c_lo > TOPK

        @pl.when(jnp.max(jnp.where(tied, 1.0, 0.0)) > 0.0)
        def _():
            keep = TOPK - c_hi
            tri = jnp.where(lax.broadcasted_iota(I32, (CK, CK), 0) >= lax.broadcasted_iota(I32, (CK, CK), 1),
                            1.0, 0.0).astype(BF16)

            def demote(c, before):
                k = keys_ref[c]
                eq = k == lo
                prefix = before + jnp.dot(tri, jnp.where(eq, 1.0, 0.0).astype(BF16),
                                          preferred_element_type=F32)
                keys_ref[c] = jnp.where(eq & tied & (prefix > keep), k - 1, k)
                return prefix[CK - 1:CK, :]

            lax.fori_loop(0, nck, demote, jnp.zeros((1, LANES), F32))

    thr = thr_ref[0:1, :]
    bq = stack_heads(bq_ref[0])
    m_ref[...] = jnp.full(m_ref.shape, NEG, F32)
    acc_ref[...] = jnp.zeros_like(acc_ref)
    lane = lax.broadcasted_iota(I32, (CK, LANES), 1)

    def attend_chunk(c, carry):
        start = pl.multiple_of(c * CK, CK)
        kc = kall_ref[0, pl.ds(start, CK), 0:LANES]
        vc = kall_ref[0, pl.ds(start, CK), LANES:2 * LANES].astype(F32)
        v_t = jnp.where(lane == HEAD_DIM, 1.0, vc).T.astype(BF16)
        s = lax.dot_general(kc, bq, NT, preferred_element_type=F32)
        dist = tq - (c * CK + rel)
        sel = (keys_ref[c] >= thr) & (dist >= 0)
        negb = jnp.where(sel, 0.0, NEG)
        distf = dist.astype(F32)
        ps, alphas = [], []
        for h in range(B_HEADS):
            l = s[:, h * LANES:(h + 1) * LANES] - SLOPES_B[h] * distf + negb
            m_old = m_ref[h:h + 1, :]
            m_new = jnp.maximum(m_old, jnp.max(l, axis=0, keepdims=True))
            m_ref[h:h + 1, :] = m_new
            alphas.append(jnp.exp(m_old - m_new))
            ps.append(jnp.exp(l - m_new).astype(BF16))
        p = jnp.concatenate(ps, axis=1)
        alpha = jnp.concatenate(alphas, axis=1)
        acc_ref[...] = acc_ref[...] * alpha + jnp.dot(v_t, p, preferred_element_type=F32)
        return carry

    lax.fori_loop(0, nck, attend_chunk, 0)

    outs = []
    for h in range(B_HEADS):
        blk = acc_ref[:, h * LANES:(h + 1) * LANES]
        outs.append((blk / blk[HEAD_DIM:HEAD_DIM + 1, :]).T[:, 0:HEAD_DIM])
    yb_ref[0] = jnp.concatenate(outs, axis=1).astype(BF16)


def _dsa(iq, iw, bq, kall):
    bsz, t, _ = iq.shape
    nb = t // BLOCK
    return pl.pallas_call(
        _dsa_kernel,
        grid=(bsz, nb),
        in_specs=[
            pl.BlockSpec((1, BLOCK, B_HEADS * LANES), lambda b, n: (b, n, 0)),
            pl.BlockSpec((1, BLOCK, LANES), lambda b, n: (b, n, 0)),
            pl.BlockSpec((1, BLOCK, B_HEADS * LANES), lambda b, n: (b, n, 0)),
            pl.BlockSpec((1, t, 2 * LANES), lambda b, n: (b, 0, 0)),
        ],
        out_specs=pl.BlockSpec((1, BLOCK, B_HEADS * HEAD_DIM), lambda b, n: (b, n, 0)),
        out_shape=jax.ShapeDtypeStruct((bsz, t, B_HEADS * HEAD_DIM), BF16),
        scratch_shapes=[
            pltpu.VMEM((t // CK, CK, LANES), I32),
            pltpu.VMEM((8, LANES), I32),
            pltpu.VMEM((B_HEADS, LANES), F32),
            pltpu.VMEM((LANES, B_HEADS * LANES), F32),
        ],
        compiler_params=_cparams("parallel", "arbitrary"),
        name="dsa",
    )(iq, iw, bq, kall)


MERGE_TM = 512


def _merge_kernel(h_ref, ya_ref, yb_ref, g_ref, wa_ref, wb_ref, wo_ref, o_ref):
    ta = jnp.dot(ya_ref[...], wa_ref[...], preferred_element_type=F32)
    tb = jnp.dot(yb_ref[...], wb_ref[...], preferred_element_type=F32)
    g = g_ref[...]
    mix = (g[:, :D_MODEL] * ta + g[:, D_MODEL:] * tb).astype(BF16)
    o_ref[...] = h_ref[...] + jnp.dot(mix, wo_ref[...], preferred_element_type=F32)


def _merge(h, ya, yb, g, wa, wb, wo):
    n = h.shape[0]
    row = lambda w: pl.BlockSpec((MERGE_TM, w), lambda i: (i, 0))
    full = lambda a: pl.BlockSpec(a.shape, lambda i: (0, 0))
    return pl.pallas_call(
        _merge_kernel,
        grid=(n // MERGE_TM,),
        in_specs=[row(D_MODEL), row(ya.shape[1]), row(yb.shape[1]), row(2 * D_MODEL),
                  full(wa), full(wb), full(wo)],
        out_specs=row(D_MODEL),
        out_shape=jax.ShapeDtypeStruct((n, D_MODEL), F32),
        compiler_params=_cparams("parallel"),
        name="merge",
    )(h, ya, yb, g, wa, wb, wo)


def _ple_kernel(h_ref, p_ref, lnp_ref, wg_ref, wp_ref, lnf_ref, o_ref):
    x = h_ref[...]
    u = _rms(x, lnp_ref[...]).astype(BF16)
    gate = jax.nn.sigmoid(jnp.dot(u, wg_ref[...], preferred_element_type=F32))
    proj = jnp.dot(p_ref[...].astype(BF16), wp_ref[...], preferred_element_type=F32)
    o_ref[...] = _rms(x + gate * proj, lnf_ref[...])


def _ple(h, p, lnp, wg, wp, lnf):
    n = h.shape[0]
    row = lambda w: pl.BlockSpec((MERGE_TM, w), lambda i: (i, 0))
    full = lambda a: pl.BlockSpec(a.shape, lambda i: (0, 0))
    return pl.pallas_call(
        _ple_kernel,
        grid=(n // MERGE_TM,),
        in_specs=[row(D_MODEL), row(D_PLE), full(lnp), full(wg), full(wp), full(lnf)],
        out_specs=row(D_MODEL),
        out_shape=jax.ShapeDtypeStruct((n, D_MODEL), F32),
        compiler_params=_cparams("parallel"),
        name="ple",
    )(h, p, lnp, wg, wp, lnf)


def kernel(x, p, ln_ffn1, w_ffn1_in, w_ffn1_out, ln_mix, w_in, a_sink, w_br_a, w_br_b, w_out,
           ln_ffn2, w_ffn2_in, w_ffn2_out, ln_ple, w_ple_gate, w_ple_proj, ln_final):
    bsz, t, d = x.shape
    assert p.shape[0] == 1, "the final norm is fused into the single layer's last kernel"
    n = bsz * t
    h = x.reshape(n, d)
    h = _ffn(h, ln_ffn1, w_ffn1_in[0].astype(BF16), w_ffn1_out[0].astype(BF16))
    aq, akv, iq, bq, kall, iw, g = _in_proj(h, ln_mix, _split_in_proj(w_in[0]))
    tok = lambda a: a.reshape(bsz, t, a.shape[-1])
    ya = _swa(a_sink[0], tok(aq), tok(akv))
    yb = _dsa(tok(iq), tok(iw), tok(bq), tok(kall))
    h = _merge(h, ya.reshape(n, -1), yb.reshape(n, -1), g,
               w_br_a[0].astype(BF16), w_br_b[0].astype(BF16), w_out[0].astype(BF16))
    h = _ffn(h, ln_ffn2, w_ffn2_in[0].astype(BF16), w_ffn2_out[0].astype(BF16))
    h = _ple(h, p[0].reshape(n, -1), ln_ple, w_ple_gate[0].astype(BF16),
             w_ple_proj[0].astype(BF16), ln_final[None])
    return h.reshape(bsz, t, d)
```

```python
import functools

import numpy as np
import jax
import jax.numpy as jnp
from jax import lax
from jax.experimental import pallas as pl
from jax.experimental.pallas import tpu as pltpu

F32 = jnp.float32
BF16 = jnp.bfloat16
I32 = jnp.int32

D_MODEL = 1024
HEAD_DIM = 64
A_HEADS = 8
A_KV = 2
BLOCK = 128
B_HEADS = 8
IDX_HEADS = 8
TOPK = 256
D_FF = 2816
D_PLE = 256
EPS = 1e-6
NEG = -1e30

LANES = 128
VMEM_LIMIT = 48 * 1024 * 1024

_SLOPES = [float(np.float32(2.0) ** np.float32(-8.0 * i / (A_HEADS + B_HEADS)))
           for i in range(1, A_HEADS + B_HEADS + 1)]
SLOPES_A = _SLOPES[:A_HEADS]
SLOPES_B = _SLOPES[A_HEADS:]
LOG2E = float(np.log2(np.e))


def _bf16_terms(x, n=3):
    out = []
    for _ in range(n):
        t = float(np.float32(x).astype(jnp.bfloat16))
        out.append(t)
        x = x - t
    return out


SLOPE2_TERMS = [_bf16_terms(s * LOG2E) for s in SLOPES_B]

NT = (((1,), (1,)), ((), ()))


def _rms(x, g):
    return x * lax.rsqrt(jnp.mean(x * x, axis=-1, keepdims=True) + EPS) * g


def _cparams(*sem):
    return pltpu.CompilerParams(dimension_semantics=sem, vmem_limit_bytes=VMEM_LIMIT)


FFN_TM = 512
FFN_TF = 1408


def _ffn_kernel(h_ref, ln_ref, wa_ref, wb_ref, wo_ref, o_ref, xn_ref, acc_ref):
    j = pl.program_id(1)

    @pl.when(j == 0)
    def _():
        xn_ref[...] = _rms(h_ref[...], ln_ref[...]).astype(BF16)
        acc_ref[...] = jnp.zeros_like(acc_ref)

    xn = xn_ref[...]
    a = jnp.dot(xn, wa_ref[...], preferred_element_type=F32)
    b = jnp.dot(xn, wb_ref[...], preferred_element_type=F32)
    g = (a * jax.nn.sigmoid(a) * b).astype(BF16)
    acc_ref[...] += jnp.dot(g, wo_ref[...], preferred_element_type=F32)

    @pl.when(j == pl.num_programs(1) - 1)
    def _():
        o_ref[...] = h_ref[...] + 0.5 * acc_ref[...]


def _ffn(h, ln, w_in, w_out):
    n = h.shape[0]
    nj = D_FF // FFN_TF
    return pl.pallas_call(
        _ffn_kernel,
        grid=(n // FFN_TM, nj),
        in_specs=[
            pl.BlockSpec((FFN_TM, D_MODEL), lambda i, j: (i, 0)),
            pl.BlockSpec((1, D_MODEL), lambda i, j: (0, 0)),
            pl.BlockSpec((D_MODEL, FFN_TF), lambda i, j: (0, j)),
            pl.BlockSpec((D_MODEL, FFN_TF), lambda i, j: (0, j + nj)),
            pl.BlockSpec((FFN_TF, D_MODEL), lambda i, j: (j, 0)),
        ],
        out_specs=pl.BlockSpec((FFN_TM, D_MODEL), lambda i, j: (i, 0)),
        out_shape=jax.ShapeDtypeStruct((n, D_MODEL), F32),
        scratch_shapes=[pltpu.VMEM((FFN_TM, D_MODEL), BF16), pltpu.VMEM((FFN_TM, D_MODEL), F32)],
        compiler_params=_cparams("parallel", "arbitrary"),
        name="ffn",
    )(h, ln, w_in, w_in, w_out)


PROJ_TM = 256


def _inproj_kernel(h_ref, ln_ref, waq, wakv, wiq, wbq, wkall, wiw, wg,
                   o_aq, o_akv, o_iq, o_bq, o_kall, o_iw, o_g):
    u = _rms(h_ref[...], ln_ref[...]).astype(BF16)

    def mm(w):
        return jnp.dot(u, w[...], preferred_element_type=F32)

    o_aq[...] = mm(waq).astype(BF16)
    o_akv[...] = mm(wakv).astype(BF16)
    o_iq[...] = mm(wiq).astype(BF16)
    o_bq[...] = mm(wbq).astype(BF16)
    o_kall[...] = mm(wkall).astype(BF16)
    o_iw[...] = mm(wiw)
    o_g[...] = jax.nn.sigmoid(mm(wg))


def _pad_heads(w, left):
    d = w.shape[0]
    w3 = w.reshape(d, -1, HEAD_DIM)
    pad = ((0, 0), (0, 0), (HEAD_DIM, 0)) if left else ((0, 0), (0, 0), (0, HEAD_DIM))
    return jnp.pad(w3, pad).reshape(d, -1)


def _split_in_proj(w):
    scale = HEAD_DIM ** -0.5
    aq, ak, av = w[:, 0:512] * scale, w[:, 512:640], w[:, 640:768]
    bq, bk, bv = w[:, 768:1280] * (scale * LOG2E), w[:, 1280:1344], w[:, 1344:1408]
    iq, ik, iw = w[:, 1408:1920] * scale, w[:, 1920:1984], w[:, 1984:1992]
    g = w[:, 1992:4040]
    z64 = jnp.zeros((w.shape[0], HEAD_DIM), w.dtype)
    return dict(
        waq=aq.astype(BF16),
        wakv=jnp.concatenate([ak, av], axis=1).astype(BF16),
        wiq=_pad_heads(iq, left=False).astype(BF16),
        wbq=_pad_heads(bq, left=True).astype(BF16),
        wkall=jnp.concatenate([ik, bk, bv, z64], axis=1).astype(BF16),
        wiw=jnp.pad(iw, ((0, 0), (0, LANES - IDX_HEADS))).astype(BF16),
        wg=g.astype(BF16),
    )


def _in_proj(h, ln, ws):
    n = h.shape[0]
    names = ["waq", "wakv", "wiq", "wbq", "wkall", "wiw", "wg"]
    widths = [ws[k].shape[1] for k in names]
    dtypes = [BF16, BF16, BF16, BF16, BF16, F32, F32]
    return pl.pallas_call(
        _inproj_kernel,
        grid=(n // PROJ_TM,),
        in_specs=[pl.BlockSpec((PROJ_TM, D_MODEL), lambda i: (i, 0)),
                  pl.BlockSpec((1, D_MODEL), lambda i: (0, 0))]
        + [pl.BlockSpec((D_MODEL, wd), lambda i: (0, 0)) for wd in widths],
        out_specs=[pl.BlockSpec((PROJ_TM, wd), lambda i: (i, 0)) for wd in widths],
        out_shape=[jax.ShapeDtypeStruct((n, wd), dt) for wd, dt in zip(widths, dtypes)],
        compiler_params=_cparams("parallel"),
        name="in_proj",
    )(h, ln, *[ws[k] for k in names])


def _swa_kernel(sink_ref, q_ref, kvp_ref, kvc_ref, o_ref):
    n = pl.program_id(1)
    q = q_ref[0]
    kvp = kvp_ref[0]
    kvc = kvc_ref[0]
    i = lax.broadcasted_iota(I32, (BLOCK, BLOCK), 0)
    j = lax.broadcasted_iota(I32, (BLOCK, BLOCK), 1)
    lower = j <= i
    dcur = (i - j).astype(F32)
    dprev = (i - j + BLOCK).astype(F32)
    prev_bias = jnp.where(n > 0, 0.0, NEG).astype(F32)
    outs = []
    for h in range(A_HEADS):
        g = h // (A_HEADS // A_KV)
        qh = q[:, h * HEAD_DIM:(h + 1) * HEAD_DIM]
        kp = kvp[:, g * HEAD_DIM:(g + 1) * HEAD_DIM]
        kc = kvc[:, g * HEAD_DIM:(g + 1) * HEAD_DIM]
        vp = kvp[:, BLOCK + g * HEAD_DIM:BLOCK + (g + 1) * HEAD_DIM]
        vc = kvc[:, BLOCK + g * HEAD_DIM:BLOCK + (g + 1) * HEAD_DIM]
        sp = lax.dot_general(qh, kp, NT, preferred_element_type=F32)
        sc = lax.dot_general(qh, kc, NT, preferred_element_type=F32)
        s = jnp.where(lower, sc - SLOPES_A[h] * dcur, sp - SLOPES_A[h] * dprev + prev_bias)
        sink = sink_ref[h]
        m = jnp.maximum(jnp.max(s, axis=1, keepdims=True), sink)
        e = jnp.exp(s - m)
        p = e / (jnp.sum(e, axis=1, keepdims=True) + jnp.exp(sink - m))
        pc = jnp.where(lower, p, 0.0).astype(BF16)
        pp = jnp.where(lower, 0.0, p).astype(BF16)
        outs.append(jnp.dot(pc, vc, preferred_element_type=F32)
                    + jnp.dot(pp, vp, preferred_element_type=F32))
    o_ref[0] = jnp.concatenate(outs, axis=1).astype(BF16)


def _swa(sink, aq, akv):
    bsz, t, _ = aq.shape
    nb = t // BLOCK
    return pl.pallas_call(
        _swa_kernel,
        grid=(bsz, nb),
        in_specs=[
            pl.BlockSpec(memory_space=pltpu.SMEM),
            pl.BlockSpec((1, BLOCK, A_HEADS * HEAD_DIM), lambda b, n: (b, n, 0)),
            pl.BlockSpec((1, BLOCK, 2 * A_KV * HEAD_DIM), lambda b, n: (b, jnp.maximum(n - 1, 0), 0)),
            pl.BlockSpec((1, BLOCK, 2 * A_KV * HEAD_DIM), lambda b, n: (b, n, 0)),
        ],
        out_specs=pl.BlockSpec((1, BLOCK, A_HEADS * HEAD_DIM), lambda b, n: (b, n, 0)),
        out_shape=jax.ShapeDtypeStruct((bsz, t, A_HEADS * HEAD_DIM), BF16),
        compiler_params=_cparams("parallel", "arbitrary"),
        name="swa",
    )(sink, aq, akv, akv)


CK = 256
SUB = 128
ACC_ROWS = 80
TINY = 1.1754944e-38


def _tree_sum(xs):
    while len(xs) > 1:
        xs = [a + b for a, b in zip(xs[0::2], xs[1::2])]
    return xs[0]


def _for_chunks(nck, unroll, body, carry):
    def group(i, carry):
        for u in range(unroll):
            carry = body(i * unroll + u, carry)
        return carry
    nfull = nck // unroll
    carry = lax.fori_loop(0, nfull, group, carry)
    return lax.fori_loop(nfull * unroll, nck, body, carry)


def _dsa_kernel(iq_ref, iw_ref, bq_ref, kall_ref, yb_ref, sc_ref, thr_ref, s_ref, acc_ref):
    n = pl.program_id(1)
    nck = n // (CK // BLOCK) + 1
    tq = n * BLOCK + lax.broadcasted_iota(I32, (CK, LANES), 1)
    rel = lax.broadcasted_iota(I32, (CK, LANES), 0)
    lane = lax.broadcasted_iota(I32, (CK, LANES), 1)

    iq = iq_ref[0]
    qi = jnp.concatenate([iq[:, h * LANES:(h + 1) * LANES] for h in range(IDX_HEADS)], axis=0)
    iw_t = (iw_ref[0] * (IDX_HEADS ** -0.5)).T

    def score_chunk(c, carry):
        smin, smax = carry
        start = pl.multiple_of(c * CK, CK)
        kc = kall_ref[0, pl.ds(start, CK), 0:LANES]
        s = lax.dot_general(kc, qi, NT, preferred_element_type=F32)
        acc = _tree_sum([jnp.maximum(s[:, h * LANES:(h + 1) * LANES], 0.0) * iw_t[h:h + 1, :]
                         for h in range(IDX_HEADS)])
        causal = c * CK + rel <= tq
        sc_ref[c] = jnp.where(causal, acc, NEG)
        smax = jnp.maximum(smax, jnp.max(jnp.where(causal, acc, NEG), axis=0, keepdims=True))
        smin = jnp.minimum(smin, jnp.min(jnp.where(causal, acc, -NEG), axis=0, keepdims=True))
        return smin, smax

    smin, smax = _for_chunks(nck, 4, score_chunk,
                             (jnp.full((1, LANES), -NEG, F32), jnp.full((1, LANES), NEG, F32)))

    thr_ref[...] = jnp.full(thr_ref.shape, NEG, F32)

    @pl.when(n >= 2)
    def _():
        def count_ge(probe):
            def body(c, cnt):
                hit = jnp.where(sc_ref[c] >= probe, 1.0, 0.0)
                return cnt + jnp.sum(hit.reshape(CK // 32, 4, 8, LANES), axis=0)
            cnt = _for_chunks(nck, 4, body, jnp.zeros((4, 8, LANES), F32))
            return jnp.sum(cnt.reshape(32, LANES), axis=0, keepdims=True)

        def probe_of(lo, hi):
            p = 0.5 * lo + 0.5 * hi
            p = jnp.where((lo < 0.0) & (hi > 0.0), 0.0, p)
            p = jnp.where((lo == 0.0) & (hi > TINY), TINY, p)
            return jnp.where((hi == 0.0) & (lo < -TINY), -TINY, p)

        def cond(st):
            lo, hi, c_lo, c_hi = st
            p = probe_of(lo, hi)
            done = (c_lo == TOPK) | (hi - lo <= TINY) | ~((p > lo) & (p < hi))
            return jnp.max(jnp.where(done, 0.0, 1.0)) > 0.0

        def step(st):
            lo, hi, c_lo, c_hi = st
            p = probe_of(lo, hi)
            c = count_ge(p)
            ge = c >= TOPK
            return (jnp.where(ge, p, lo), jnp.where(ge, hi, p),
                    jnp.where(ge, c, c_lo), jnp.where(ge, c_hi, c))

        n_causal = (tq[0:1, :] + 1).astype(F32)
        st0 = (smin, smax + (jnp.abs(smax) * 1e-6 + TINY), n_causal, jnp.zeros((1, LANES), F32))
        lo, hi, c_lo, c_hi = lax.while_loop(cond, lambda st: step(step(st)), st0)
        thr_ref[...] = jnp.broadcast_to(lo, thr_ref.shape)

        tied = c_lo > TOPK

        @pl.when(jnp.max(jnp.where(tied, 1.0, 0.0)) > 0.0)
        def _():
            keep = TOPK - c_hi
            tri = jnp.where(lax.broadcasted_iota(I32, (CK, CK), 0) >= lax.broadcasted_iota(I32, (CK, CK), 1),
                            1.0, 0.0).astype(BF16)

            def drop(c, before):
                x = sc_ref[c]
                eq = x == lo
                prefix = before + jnp.dot(tri, jnp.where(eq, 1.0, 0.0).astype(BF16),
                                          preferred_element_type=F32)
                sc_ref[c] = jnp.where(eq & tied & (prefix > keep), NEG, x)
                return prefix[CK - 1:CK, :]

            lax.fori_loop(0, nck, drop, jnp.zeros((1, LANES), F32))

    thr = thr_ref[0:1, :]
    bq = bq_ref[0]
    aux_lanes = range(HEAD_DIM, HEAD_DIM + 3)
    lane_q = lax.broadcasted_iota(I32, (BLOCK, LANES), 1)
    q_rows = []
    for h in range(B_HEADS):
        aux = jnp.zeros((BLOCK, LANES), F32)
        for ln, term in zip(aux_lanes, SLOPE2_TERMS[h]):
            aux = jnp.where(lane_q == ln, term, aux)
        q_rows.append(jnp.concatenate([bq[:, h * LANES:(h + 1) * LANES], aux.astype(BF16)], axis=1))
    q2 = jnp.concatenate(q_rows, axis=0)
    rel_s = lax.broadcasted_iota(I32, (SUB, LANES), 0)
    lane_s = lax.broadcasted_iota(I32, (SUB, LANES), 1)
    tq_s = n * BLOCK + lane_s
    in_aux = (lane_s >= aux_lanes[0]) & (lane_s <= aux_lanes[-1])
    rel_k = jnp.where(in_aux, rel_s.astype(F32), 0.0).astype(BF16)
    slope2 = [float(np.float32(sum(t))) for t in SLOPE2_TERMS]

    acc_ref[...] = jnp.zeros_like(acc_ref)

    def kv_rows(t):
        start = pl.multiple_of(t * SUB, SUB)
        return (kall_ref[0, pl.ds(start, SUB), 0:LANES], kall_ref[0, pl.ds(start, SUB), LANES:2 * LANES])

    def stage_logits(t, slot):
        k_lo, v_hi = kv_rows(t)
        kc = jnp.concatenate([k_lo, v_hi + rel_k], axis=1)
        s_ref[slot] = lax.dot_general(kc, q2, NT, preferred_element_type=F32)

    def softmax_pv(t, slot, scores, m):
        _, v_hi = kv_rows(t)
        v_t = jnp.where(lane_s == HEAD_DIM, 1.0, v_hi.astype(F32)).T[0:ACC_ROWS].astype(BF16)
        sel = (scores >= thr) & (t * SUB + rel_s <= tq_s)
        negb = jnp.where(sel, 0.0, NEG)
        base = (jnp.zeros((1, LANES), I32) + t * SUB).astype(F32)
        ps, alphas, m_out = [], [], []
        for h in range(B_HEADS):
            l = s_ref[slot, :, h * LANES:(h + 1) * LANES] + negb
            off = base * slope2[h]
            m_new = jnp.maximum(m[h], jnp.max(l, axis=0, keepdims=True) + off)
            alphas.append(jnp.exp2(m[h] - m_new))
            ps.append(jnp.exp2(l - (m_new - off)).astype(BF16))
            m_out.append(m_new)
        pv = jnp.dot(v_t, jnp.concatenate(ps, axis=1), preferred_element_type=F32)
        acc_ref[...] = acc_ref[...] * jnp.concatenate(alphas, axis=1) + pv
        return m_out

    def step(t, slot, scores, m):
        stage_logits(jnp.minimum(t + 1, n), 1 - slot)
        return softmax_pv(t, slot, scores, m)

    def two_steps(i, m):
        m = step(2 * i, 0, sc_ref[i, 0:SUB, :], m)
        return step(2 * i + 1, 1, sc_ref[i, SUB:2 * SUB, :], m)

    def last_step(t, m):
        return step(t, 0, sc_ref[t // 2, 0:SUB, :], m)

    stage_logits(0, 0)
    m0 = [jnp.full((1, LANES), NEG, F32) for _ in range(B_HEADS)]
    m1 = lax.fori_loop(0, (n + 1) // 2, two_steps, m0)
    lax.fori_loop((n + 1) // 2 * 2, n + 1, last_step, m1)

    outs = []
    pad = jnp.zeros((LANES - HEAD_DIM, LANES), F32)
    for h in range(B_HEADS):
        blk = acc_ref[:, h * LANES:(h + 1) * LANES]
        o_t = blk[0:HEAD_DIM] / blk[HEAD_DIM:HEAD_DIM + 1]
        outs.append(jnp.concatenate([o_t, pad], axis=0).T[:, 0:HEAD_DIM])
    yb_ref[0] = jnp.concatenate(outs, axis=1).astype(BF16)


def _dsa(iq, iw, bq, kall):
    bsz, t, _ = iq.shape
    nb = t // BLOCK
    return pl.pallas_call(
        _dsa_kernel,
        grid=(bsz, nb),
        in_specs=[
            pl.BlockSpec((1, BLOCK, B_HEADS * LANES), lambda b, n: (b, n, 0)),
            pl.BlockSpec((1, BLOCK, LANES), lambda b, n: (b, n, 0)),
            pl.BlockSpec((1, BLOCK, B_HEADS * LANES), lambda b, n: (b, n, 0)),
            pl.BlockSpec((1, t, 2 * LANES), lambda b, n: (b, 0, 0)),
        ],
        out_specs=pl.BlockSpec((1, BLOCK, B_HEADS * HEAD_DIM), lambda b, n: (b, n, 0)),
        out_shape=jax.ShapeDtypeStruct((bsz, t, B_HEADS * HEAD_DIM), BF16),
        scratch_shapes=[
            pltpu.VMEM((t // CK, CK, LANES), F32),
            pltpu.VMEM((8, LANES), F32),
            pltpu.VMEM((2, SUB, B_HEADS * LANES), F32),
            pltpu.VMEM((ACC_ROWS, B_HEADS * LANES), F32),
        ],
        compiler_params=_cparams("parallel", "arbitrary"),
        name="dsa",
    )(iq, iw, bq, kall)


MERGE_TM = 512


def _merge_kernel(h_ref, ya_ref, yb_ref, g_ref, wa_ref, wb_ref, wo_ref, o_ref):
    ta = jnp.dot(ya_ref[...], wa_ref[...], preferred_element_type=F32)
    tb = jnp.dot(yb_ref[...], wb_ref[...], preferred_element_type=F32)
    g = g_ref[...]
    mix = (g[:, :D_MODEL] * ta + g[:, D_MODEL:] * tb).astype(BF16)
    o_ref[...] = h_ref[...] + jnp.dot(mix, wo_ref[...], preferred_element_type=F32)


def _merge(h, ya, yb, g, wa, wb, wo):
    n = h.shape[0]
    row = lambda w: pl.BlockSpec((MERGE_TM, w), lambda i: (i, 0))
    full = lambda a: pl.BlockSpec(a.shape, lambda i: (0, 0))
    return pl.pallas_call(
        _merge_kernel,
        grid=(n // MERGE_TM,),
        in_specs=[row(D_MODEL), row(ya.shape[1]), row(yb.shape[1]), row(2 * D_MODEL),
                  full(wa), full(wb), full(wo)],
        out_specs=row(D_MODEL),
        out_shape=jax.ShapeDtypeStruct((n, D_MODEL), F32),
        compiler_params=_cparams("parallel"),
        name="merge",
    )(h, ya, yb, g, wa, wb, wo)


def _ple_kernel(h_ref, p_ref, lnp_ref, wg_ref, wp_ref, lnf_ref, o_ref):
    x = h_ref[...]
    u = _rms(x, lnp_ref[...]).astype(BF16)
    gate = jax.nn.sigmoid(jnp.dot(u, wg_ref[...], preferred_element_type=F32))
    proj = jnp.dot(p_ref[...].astype(BF16), wp_ref[...], preferred_element_type=F32)
    o_ref[...] = _rms(x + gate * proj, lnf_ref[...])


def _ple(h, p, lnp, wg, wp, lnf):
    n = h.shape[0]
    row = lambda w: pl.BlockSpec((MERGE_TM, w), lambda i: (i, 0))
    full = lambda a: pl.BlockSpec(a.shape, lambda i: (0, 0))
    return pl.pallas_call(
        _ple_kernel,
        grid=(n // MERGE_TM,),
        in_specs=[row(D_MODEL), row(D_PLE), full(lnp), full(wg), full(wp), full(lnf)],
        out_specs=row(D_MODEL),
        out_shape=jax.ShapeDtypeStruct((n, D_MODEL), F32),
        compiler_params=_cparams("parallel"),
        name="ple",
    )(h, p, lnp, wg, wp, lnf)


def kernel(x, p, ln_ffn1, w_ffn1_in, w_ffn1_out, ln_mix, w_in, a_sink, w_br_a, w_br_b, w_out,
           ln_ffn2, w_ffn2_in, w_ffn2_out, ln_ple, w_ple_gate, w_ple_proj, ln_final):
    bsz, t, d = x.shape
    assert p.shape[0] == 1, "the final norm is fused into the single layer's last kernel"
    n = bsz * t
    h = x.reshape(n, d)
    h = _ffn(h, ln_ffn1, w_ffn1_in[0].astype(BF16), w_ffn1_out[0].astype(BF16))
    aq, akv, iq, bq, kall, iw, g = _in_proj(h, ln_mix, _split_in_proj(w_in[0]))
    tok = lambda a: a.reshape(bsz, t, a.shape[-1])
    ya = _swa(a_sink[0], tok(aq), tok(akv))
    yb = _dsa(tok(iq), tok(iw), tok(bq), tok(kall))
    h = _merge(h, ya.reshape(n, -1), yb.reshape(n, -1), g,
               w_br_a[0].astype(BF16), w_br_b[0].astype(BF16), w_out[0].astype(BF16))
    h = _ffn(h, ln_ffn2, w_ffn2_in[0].astype(BF16), w_ffn2_out[0].astype(BF16))
    h = _ple(h, p[0].reshape(n, -1), ln_ple, w_ple_gate[0].astype(BF16),
             w_ple_proj[0].astype(BF16), ln_final[None])
    return h.reshape(bsz, t, d)
```

```python
import functools

import numpy as np
import jax
import jax.numpy as jnp
from jax import lax
from jax.experimental import pallas as pl
from jax.experimental.pallas import tpu as pltpu

F32 = jnp.float32
BF16 = jnp.bfloat16
I32 = jnp.int32

D_MODEL = 1024
HEAD_DIM = 64
A_HEADS = 8
A_KV = 2
BLOCK = 128
B_HEADS = 8
IDX_HEADS = 8
TOPK = 256
D_FF = 2816
D_PLE = 256
EPS = 1e-6
NEG = -1e30

LANES = 128
VMEM_LIMIT = 48 * 1024 * 1024

_SLOPES = [float(np.float32(2.0) ** np.float32(-8.0 * i / (A_HEADS + B_HEADS)))
           for i in range(1, A_HEADS + B_HEADS + 1)]
SLOPES_A = _SLOPES[:A_HEADS]
SLOPES_B = _SLOPES[A_HEADS:]
LOG2E = float(np.log2(np.e))


def _bf16_terms(x, n=3):
    out = []
    for _ in range(n):
        t = float(np.float32(x).astype(jnp.bfloat16))
        out.append(t)
        x = x - t
    return out


SLOPE2_TERMS = [_bf16_terms(s * LOG2E) for s in SLOPES_B]

NT = (((1,), (1,)), ((), ()))


def _rms(x, g):
    return x * lax.rsqrt(jnp.mean(x * x, axis=-1, keepdims=True) + EPS) * g


def _cparams(*sem):
    return pltpu.CompilerParams(dimension_semantics=sem, vmem_limit_bytes=VMEM_LIMIT)


FFN_TM = 512
FFN_TF = 1408


def _ffn_kernel(h_ref, ln_ref, wa_ref, wb_ref, wo_ref, o_ref, xn_ref, acc_ref):
    j = pl.program_id(1)

    @pl.when(j == 0)
    def _():
        xn_ref[...] = _rms(h_ref[...], ln_ref[...]).astype(BF16)
        acc_ref[...] = jnp.zeros_like(acc_ref)

    xn = xn_ref[...]
    a = jnp.dot(xn, wa_ref[...], preferred_element_type=F32)
    b = jnp.dot(xn, wb_ref[...], preferred_element_type=F32)
    g = (a * jax.nn.sigmoid(a) * b).astype(BF16)
    acc_ref[...] += jnp.dot(g, wo_ref[...], preferred_element_type=F32)

    @pl.when(j == pl.num_programs(1) - 1)
    def _():
        o_ref[...] = h_ref[...] + 0.5 * acc_ref[...]


def _ffn(h, ln, w_in, w_out):
    n = h.shape[0]
    nj = D_FF // FFN_TF
    return pl.pallas_call(
        _ffn_kernel,
        grid=(n // FFN_TM, nj),
        in_specs=[
            pl.BlockSpec((FFN_TM, D_MODEL), lambda i, j: (i, 0)),
            pl.BlockSpec((1, D_MODEL), lambda i, j: (0, 0)),
            pl.BlockSpec((D_MODEL, FFN_TF), lambda i, j: (0, j)),
            pl.BlockSpec((D_MODEL, FFN_TF), lambda i, j: (0, j + nj)),
            pl.BlockSpec((FFN_TF, D_MODEL), lambda i, j: (j, 0)),
        ],
        out_specs=pl.BlockSpec((FFN_TM, D_MODEL), lambda i, j: (i, 0)),
        out_shape=jax.ShapeDtypeStruct((n, D_MODEL), F32),
        scratch_shapes=[pltpu.VMEM((FFN_TM, D_MODEL), BF16), pltpu.VMEM((FFN_TM, D_MODEL), F32)],
        compiler_params=_cparams("parallel", "arbitrary"),
        name="ffn",
    )(h, ln, w_in, w_in, w_out)


PROJ_TM = 256


def _inproj_kernel(h_ref, ln_ref, waq, wakv, wiq, wbq, wki, wkb, wv, wiw, wg,
                   o_aq, o_akv, o_iqt, o_bqt, o_ki, o_kb, o_vt, o_iw, o_g):
    u = _rms(h_ref[...], ln_ref[...]).astype(BF16)

    def mm(w):
        return jnp.dot(u, w[...], preferred_element_type=F32)

    o_aq[...] = mm(waq).astype(BF16)
    o_akv[...] = mm(wakv).astype(BF16)
    o_iw[...] = mm(wiw)
    o_g[...] = jax.nn.sigmoid(mm(wg))

    row = lax.broadcasted_iota(I32, (PROJ_TM, LANES), 0)
    lane = lax.broadcasted_iota(I32, (PROJ_TM, LANES), 1)
    rel = jnp.where((lane >= HEAD_DIM) & (lane < HEAD_DIM + 3), (row % BLOCK).astype(F32), 0.0)
    o_ki[...] = mm(wki).astype(BF16)
    o_kb[...] = (mm(wkb) + rel).astype(BF16)
    v1 = jnp.where(lane == HEAD_DIM, 1.0, mm(wv))
    iq = mm(wiq)
    bq = mm(wbq)
    row64 = lax.broadcasted_iota(I32, (HEAD_DIM, BLOCK), 0)
    zeros64 = jnp.zeros((HEAD_DIM, BLOCK), F32)
    for blk in range(PROJ_TM // BLOCK):
        tok = slice(blk * BLOCK, (blk + 1) * BLOCK)
        o_vt[blk] = v1[tok].T.astype(BF16)
        for j in range(B_HEADS // 2):
            iq_t = iq[tok, j * LANES:(j + 1) * LANES].T
            bq_t = bq[tok, j * LANES:(j + 1) * LANES].T
            for hh in range(2):
                h = 2 * j + hh
                cols = slice(h * BLOCK, (h + 1) * BLOCK)
                part = slice(hh * HEAD_DIM, (hh + 1) * HEAD_DIM)
                o_iqt[blk, :, cols] = jnp.concatenate([iq_t[part], zeros64], axis=0).astype(BF16)
                slope_rows = zeros64
                for r, term in enumerate(SLOPE2_TERMS[h]):
                    slope_rows = jnp.where(row64 == r, term, slope_rows)
                o_bqt[blk, :, cols] = jnp.concatenate([bq_t[part], slope_rows], axis=0).astype(BF16)


def _split_in_proj(w):
    scale = HEAD_DIM ** -0.5
    aq, ak, av = w[:, 0:512] * scale, w[:, 512:640], w[:, 640:768]
    bq, bk, bv = w[:, 768:1280] * (scale * LOG2E), w[:, 1280:1344], w[:, 1344:1408]
    iq, ik, iw = w[:, 1408:1920] * scale, w[:, 1920:1984], w[:, 1984:1992]
    g = w[:, 1992:4040]
    z64 = jnp.zeros((w.shape[0], HEAD_DIM), w.dtype)
    return dict(
        waq=aq.astype(BF16),
        wakv=jnp.concatenate([ak, av], axis=1).astype(BF16),
        wiq=iq.astype(BF16),
        wbq=bq.astype(BF16),
        wki=jnp.concatenate([ik, z64], axis=1).astype(BF16),
        wkb=jnp.concatenate([bk, z64], axis=1).astype(BF16),
        wv=jnp.concatenate([bv, z64], axis=1).astype(BF16),
        wiw=jnp.pad(iw, ((0, 0), (0, LANES - IDX_HEADS))).astype(BF16),
        wg=g.astype(BF16),
    )


def _in_proj(h, ln, ws):
    n = h.shape[0]
    blocks = PROJ_TM // BLOCK
    names = ["waq", "wakv", "wiq", "wbq", "wki", "wkb", "wv", "wiw", "wg"]
    rows = lambda wd, dt: (pl.BlockSpec((PROJ_TM, wd), lambda i: (i, 0)), jax.ShapeDtypeStruct((n, wd), dt))
    tiles = lambda r, c: (pl.BlockSpec((blocks, r, c), lambda i: (i, 0, 0)),
                          jax.ShapeDtypeStruct((n // BLOCK, r, c), BF16))
    outs = [rows(A_HEADS * HEAD_DIM, BF16), rows(2 * A_KV * HEAD_DIM, BF16),
            tiles(LANES, IDX_HEADS * BLOCK), tiles(LANES, B_HEADS * BLOCK),
            rows(LANES, BF16), rows(LANES, BF16), tiles(LANES, BLOCK),
            rows(LANES, F32), rows(2 * D_MODEL, F32)]
    return pl.pallas_call(
        _inproj_kernel,
        grid=(n // PROJ_TM,),
        in_specs=[pl.BlockSpec((PROJ_TM, D_MODEL), lambda i: (i, 0)),
                  pl.BlockSpec((1, D_MODEL), lambda i: (0, 0))]
        + [pl.BlockSpec((D_MODEL, ws[k].shape[1]), lambda i: (0, 0)) for k in names],
        out_specs=[o[0] for o in outs],
        out_shape=[o[1] for o in outs],
        compiler_params=_cparams("parallel"),
        name="in_proj",
    )(h, ln, *[ws[k] for k in names])


def _swa_kernel(sink_ref, q_ref, kvp_ref, kvc_ref, o_ref):
    n = pl.program_id(1)
    q = q_ref[0]
    kvp = kvp_ref[0]
    kvc = kvc_ref[0]
    i = lax.broadcasted_iota(I32, (BLOCK, BLOCK), 0)
    j = lax.broadcasted_iota(I32, (BLOCK, BLOCK), 1)
    lower = j <= i
    dcur = (i - j).astype(F32)
    dprev = (i - j + BLOCK).astype(F32)
    prev_bias = jnp.where(n > 0, 0.0, NEG).astype(F32)
    outs = []
    for h in range(A_HEADS):
        g = h // (A_HEADS // A_KV)
        qh = q[:, h * HEAD_DIM:(h + 1) * HEAD_DIM]
        kp = kvp[:, g * HEAD_DIM:(g + 1) * HEAD_DIM]
        kc = kvc[:, g * HEAD_DIM:(g + 1) * HEAD_DIM]
        vp = kvp[:, BLOCK + g * HEAD_DIM:BLOCK + (g + 1) * HEAD_DIM]
        vc = kvc[:, BLOCK + g * HEAD_DIM:BLOCK + (g + 1) * HEAD_DIM]
        sp = lax.dot_general(qh, kp, NT, preferred_element_type=F32)
        sc = lax.dot_general(qh, kc, NT, preferred_element_type=F32)
        s = jnp.where(lower, sc - SLOPES_A[h] * dcur, sp - SLOPES_A[h] * dprev + prev_bias)
        sink = sink_ref[h]
        m = jnp.maximum(jnp.max(s, axis=1, keepdims=True), sink)
        e = jnp.exp(s - m)
        p = e / (jnp.sum(e, axis=1, keepdims=True) + jnp.exp(sink - m))
        pc = jnp.where(lower, p, 0.0).astype(BF16)
        pp = jnp.where(lower, 0.0, p).astype(BF16)
        outs.append(jnp.dot(pc, vc, preferred_element_type=F32)
                    + jnp.dot(pp, vp, preferred_element_type=F32))
    o_ref[0] = jnp.concatenate(outs, axis=1).astype(BF16)


def _swa(sink, aq, akv):
    bsz, t, _ = aq.shape
    nb = t // BLOCK
    return pl.pallas_call(
        _swa_kernel,
        grid=(bsz, nb),
        in_specs=[
            pl.BlockSpec(memory_space=pltpu.SMEM),
            pl.BlockSpec((1, BLOCK, A_HEADS * HEAD_DIM), lambda b, n: (b, n, 0)),
            pl.BlockSpec((1, BLOCK, 2 * A_KV * HEAD_DIM), lambda b, n: (b, jnp.maximum(n - 1, 0), 0)),
            pl.BlockSpec((1, BLOCK, 2 * A_KV * HEAD_DIM), lambda b, n: (b, n, 0)),
        ],
        out_specs=pl.BlockSpec((1, BLOCK, A_HEADS * HEAD_DIM), lambda b, n: (b, n, 0)),
        out_shape=jax.ShapeDtypeStruct((bsz, t, A_HEADS * HEAD_DIM), BF16),
        compiler_params=_cparams("parallel", "arbitrary"),
        name="swa",
    )(sink, aq, akv, akv)


CK = 256
SUB = 128
PIPE = 4
ACC_ROWS = 80
TINY = 1.1754944e-38


def _tree_sum(xs):
    while len(xs) > 1:
        xs = [a + b for a, b in zip(xs[0::2], xs[1::2])]
    return xs[0]


def _for_chunks(nck, unroll, body, carry):
    def group(i, carry):
        for u in range(unroll):
            carry = body(i * unroll + u, carry)
        return carry
    nfull = nck // unroll
    carry = lax.fori_loop(0, nfull, group, carry)
    return lax.fori_loop(nfull * unroll, nck, body, carry)


def _dsa_kernel(iqt_ref, iw_ref, bqt_ref, ki_ref, kb_ref, vt_ref, yb_ref, sc_ref, thr_ref, s_ref, p_ref,
                alpha_ref, acc_ref):
    n = pl.program_id(1)
    nck = n // (CK // BLOCK) + 1
    tq = n * BLOCK + lax.broadcasted_iota(I32, (CK, LANES), 1)
    rel = lax.broadcasted_iota(I32, (CK, LANES), 0)

    qi_t = iqt_ref[0, 0]
    iw_t = (iw_ref[0] * (IDX_HEADS ** -0.5)).T

    def score_chunk(c, carry):
        smin, smax = carry
        start = pl.multiple_of(c * CK, CK)
        s = jnp.dot(ki_ref[0, pl.ds(start, CK), :], qi_t, preferred_element_type=F32)
        acc = _tree_sum([jnp.maximum(s[:, h * LANES:(h + 1) * LANES], 0.0) * iw_t[h:h + 1, :]
                         for h in range(IDX_HEADS)])
        causal = c * CK + rel <= tq
        sc_ref[c] = jnp.where(causal, acc, NEG)
        smax = jnp.maximum(smax, jnp.max(jnp.where(causal, acc, NEG), axis=0, keepdims=True))
        smin = jnp.minimum(smin, jnp.min(jnp.where(causal, acc, -NEG), axis=0, keepdims=True))
        return smin, smax

    smin, smax = _for_chunks(nck, 4, score_chunk,
                             (jnp.full((1, LANES), -NEG, F32), jnp.full((1, LANES), NEG, F32)))

    thr_ref[...] = jnp.full(thr_ref.shape, NEG, F32)

    @pl.when(n >= 2)
    def _():
        def count_ge(probe):
            def body(c, cnt):
                hit = jnp.where(sc_ref[c] >= probe, 1.0, 0.0)
                return cnt + jnp.sum(hit.reshape(CK // 32, 4, 8, LANES), axis=0)
            cnt = _for_chunks(nck, 4, body, jnp.zeros((4, 8, LANES), F32))
            return jnp.sum(cnt.reshape(32, LANES), axis=0, keepdims=True)

        def probe_of(lo, hi):
            p = 0.5 * lo + 0.5 * hi
            p = jnp.where((lo < 0.0) & (hi > 0.0), 0.0, p)
            p = jnp.where((lo == 0.0) & (hi > TINY), TINY, p)
            return jnp.where((hi == 0.0) & (lo < -TINY), -TINY, p)

        def cond(st):
            lo, hi, c_lo, c_hi = st
            p = probe_of(lo, hi)
            done = (c_lo == TOPK) | (hi - lo <= TINY) | ~((p > lo) & (p < hi))
            return jnp.max(jnp.where(done, 0.0, 1.0)) > 0.0

        def step(st):
            lo, hi, c_lo, c_hi = st
            p = probe_of(lo, hi)
            c = count_ge(p)
            ge = c >= TOPK
            return (jnp.where(ge, p, lo), jnp.where(ge, hi, p),
                    jnp.where(ge, c, c_lo), jnp.where(ge, c_hi, c))

        n_causal = (tq[0:1, :] + 1).astype(F32)
        st0 = (smin, smax + (jnp.abs(smax) * 1e-6 + TINY), n_causal, jnp.zeros((1, LANES), F32))
        lo, hi, c_lo, c_hi = lax.while_loop(cond, lambda st: step(step(st)), st0)
        thr_ref[...] = jnp.broadcast_to(lo, thr_ref.shape)

        tied = c_lo > TOPK

        @pl.when(jnp.max(jnp.where(tied, 1.0, 0.0)) > 0.0)
        def _():
            keep = TOPK - c_hi
            tri = jnp.where(lax.broadcasted_iota(I32, (CK, CK), 0) >= lax.broadcasted_iota(I32, (CK, CK), 1),
                            1.0, 0.0).astype(BF16)

            def drop(c, before):
                x = sc_ref[c]
                eq = x == lo
                prefix = before + jnp.dot(tri, jnp.where(eq, 1.0, 0.0).astype(BF16),
                                          preferred_element_type=F32)
                sc_ref[c] = jnp.where(eq & tied & (prefix > keep), NEG, x)
                return prefix[CK - 1:CK, :]

            lax.fori_loop(0, nck, drop, jnp.zeros((1, LANES), F32))

    thr = thr_ref[0:1, :]
    q2_t = bqt_ref[0, 0]
    rel_s = lax.broadcasted_iota(I32, (SUB, LANES), 0)
    tq_s = n * BLOCK + lax.broadcasted_iota(I32, (SUB, LANES), 1)
    slope2 = [float(np.float32(sum(t))) for t in SLOPE2_TERMS]

    acc_ref[...] = jnp.zeros_like(acc_ref)

    last = pl.num_programs(1) - 1

    def stage_logits(t, k):
        start = pl.multiple_of(jnp.minimum(t, last) * SUB, SUB)
        s_ref[k] = jnp.dot(kb_ref[0, pl.ds(start, SUB), :], q2_t, preferred_element_type=F32)

    def softmax(t, k, m):
        tc = jnp.minimum(t, n)
        rows = pl.ds(pl.multiple_of((tc % 2) * SUB, SUB), SUB)
        sel = (sc_ref[tc // 2, rows, :] >= thr) & (t * SUB + rel_s <= tq_s)
        negb = jnp.where(sel, 0.0, NEG)
        base = (jnp.zeros((1, LANES), I32) + t * SUB).astype(F32)
        m_out = []
        for h in range(B_HEADS):
            l = s_ref[k, :, h * LANES:(h + 1) * LANES] + negb
            off = base * slope2[h]
            m_new = jnp.maximum(m[h], jnp.max(l, axis=0, keepdims=True) + off)
            alpha_ref[k, h:h + 1, :] = jnp.exp2(m[h] - m_new)
            p_ref[k, :, h * LANES:(h + 1) * LANES] = jnp.exp2(l - (m_new - off)).astype(BF16)
            m_out.append(m_new)
        return m_out

    def pv_accumulate(t, k):
        v_t = vt_ref[0, jnp.clip(t, 0, last), 0:ACC_ROWS, :]
        pv = jnp.dot(v_t, p_ref[k], preferred_element_type=F32)
        alpha = jnp.concatenate([alpha_ref[k, h:h + 1, :] for h in range(B_HEADS)], axis=1)
        acc_ref[...] = acc_ref[...] * alpha + pv

    def group(g, m):
        for k in range(PIPE):
            pv_accumulate((g - 1) * PIPE + k, k)
        for k in range(PIPE):
            m = softmax(g * PIPE + k, k, m)
        for k in range(PIPE):
            stage_logits((g + 1) * PIPE + k, k)
        return m

    ngroups = n // PIPE + 1
    for k in range(PIPE):
        stage_logits(k, k)
    p_ref[...] = jnp.zeros(p_ref.shape, BF16)
    alpha_ref[...] = jnp.ones(alpha_ref.shape, F32)
    lax.fori_loop(0, ngroups, group, [jnp.full((1, LANES), NEG, F32) for _ in range(B_HEADS)])
    for k in range(PIPE):
        pv_accumulate((ngroups - 1) * PIPE + k, k)

    outs = []
    pad = jnp.zeros((LANES - HEAD_DIM, LANES), F32)
    for h in range(B_HEADS):
        blk = acc_ref[:, h * LANES:(h + 1) * LANES]
        o_t = blk[0:HEAD_DIM] / blk[HEAD_DIM:HEAD_DIM + 1]
        outs.append(jnp.concatenate([o_t, pad], axis=0).T[:, 0:HEAD_DIM])
    yb_ref[0] = jnp.concatenate(outs, axis=1).astype(BF16)


def _dsa(iqt, iw, bqt, ki, kb, vt):
    bsz, t, _ = iw.shape
    nb = t // BLOCK
    return pl.pallas_call(
        _dsa_kernel,
        grid=(bsz, nb),
        in_specs=[
            pl.BlockSpec((1, 1, LANES, IDX_HEADS * BLOCK), lambda b, n: (b, n, 0, 0)),
            pl.BlockSpec((1, BLOCK, LANES), lambda b, n: (b, n, 0)),
            pl.BlockSpec((1, 1, LANES, B_HEADS * BLOCK), lambda b, n: (b, n, 0, 0)),
            pl.BlockSpec((1, t, LANES), lambda b, n: (b, 0, 0)),
            pl.BlockSpec((1, t, LANES), lambda b, n: (b, 0, 0)),
            pl.BlockSpec((1, nb, LANES, BLOCK), lambda b, n: (b, 0, 0, 0)),
        ],
        out_specs=pl.BlockSpec((1, BLOCK, B_HEADS * HEAD_DIM), lambda b, n: (b, n, 0)),
        out_shape=jax.ShapeDtypeStruct((bsz, t, B_HEADS * HEAD_DIM), BF16),
        scratch_shapes=[
            pltpu.VMEM((t // CK, CK, LANES), F32),
            pltpu.VMEM((8, LANES), F32),
            pltpu.VMEM((PIPE, SUB, B_HEADS * LANES), F32),
            pltpu.VMEM((PIPE, SUB, B_HEADS * LANES), BF16),
            pltpu.VMEM((PIPE, B_HEADS, LANES), F32),
            pltpu.VMEM((ACC_ROWS, B_HEADS * LANES), F32),
        ],
        compiler_params=_cparams("parallel", "arbitrary"),
        name="dsa",
    )(iqt, iw, bqt, ki, kb, vt)


MERGE_TM = 512


def _merge_kernel(h_ref, ya_ref, yb_ref, g_ref, wa_ref, wb_ref, wo_ref, o_ref):
    ta = jnp.dot(ya_ref[...], wa_ref[...], preferred_element_type=F32)
    tb = jnp.dot(yb_ref[...], wb_ref[...], preferred_element_type=F32)
    g = g_ref[...]
    mix = (g[:, :D_MODEL] * ta + g[:, D_MODEL:] * tb).astype(BF16)
    o_ref[...] = h_ref[...] + jnp.dot(mix, wo_ref[...], preferred_element_type=F32)


def _merge(h, ya, yb, g, wa, wb, wo):
    n = h.shape[0]
    row = lambda w: pl.BlockSpec((MERGE_TM, w), lambda i: (i, 0))
    full = lambda a: pl.BlockSpec(a.shape, lambda i: (0, 0))
    return pl.pallas_call(
        _merge_kernel,
        grid=(n // MERGE_TM,),
        in_specs=[row(D_MODEL), row(ya.shape[1]), row(yb.shape[1]), row(2 * D_MODEL),
                  full(wa), full(wb), full(wo)],
        out_specs=row(D_MODEL),
        out_shape=jax.ShapeDtypeStruct((n, D_MODEL), F32),
        compiler_params=_cparams("parallel"),
        name="merge",
    )(h, ya, yb, g, wa, wb, wo)


def _ple_kernel(h_ref, p_ref, lnp_ref, wg_ref, wp_ref, lnf_ref, o_ref):
    x = h_ref[...]
    u = _rms(x, lnp_ref[...]).astype(BF16)
    gate = jax.nn.sigmoid(jnp.dot(u, wg_ref[...], preferred_element_type=F32))
    proj = jnp.dot(p_ref[...].astype(BF16), wp_ref[...], preferred_element_type=F32)
    o_ref[...] = _rms(x + gate * proj, lnf_ref[...])


def _ple(h, p, lnp, wg, wp, lnf):
    n = h.shape[0]
    row = lambda w: pl.BlockSpec((MERGE_TM, w), lambda i: (i, 0))
    full = lambda a: pl.BlockSpec(a.shape, lambda i: (0, 0))
    return pl.pallas_call(
        _ple_kernel,
        grid=(n // MERGE_TM,),
        in_specs=[row(D_MODEL), row(D_PLE), full(lnp), full(wg), full(wp), full(lnf)],
        out_specs=row(D_MODEL),
        out_shape=jax.ShapeDtypeStruct((n, D_MODEL), F32),
        compiler_params=_cparams("parallel"),
        name="ple",
    )(h, p, lnp, wg, wp, lnf)


def kernel(x, p, ln_ffn1, w_ffn1_in, w_ffn1_out, ln_mix, w_in, a_sink, w_br_a, w_br_b, w_out,
           ln_ffn2, w_ffn2_in, w_ffn2_out, ln_ple, w_ple_gate, w_ple_proj, ln_final):
    bsz, t, d = x.shape
    assert p.shape[0] == 1, "the final norm is fused into the single layer's last kernel"
    n = bsz * t
    h = x.reshape(n, d)
    h = _ffn(h, ln_ffn1, w_ffn1_in[0].astype(BF16), w_ffn1_out[0].astype(BF16))
    aq, akv, iqt, bqt, ki, kb, vt, iw, g = _in_proj(h, ln_mix, _split_in_proj(w_in[0]))
    tok = lambda a: a.reshape(bsz, t, a.shape[-1])
    ya = _swa(a_sink[0], tok(aq), tok(akv))
    blk = lambda a: a.reshape(bsz, t // BLOCK, a.shape[-2], a.shape[-1])
    yb = _dsa(blk(iqt), tok(iw), blk(bqt), tok(ki), tok(kb), blk(vt))
    h = _merge(h, ya.reshape(n, -1), yb.reshape(n, -1), g,
               w_br_a[0].astype(BF16), w_br_b[0].astype(BF16), w_out[0].astype(BF16))
    h = _ffn(h, ln_ffn2, w_ffn2_in[0].astype(BF16), w_ffn2_out[0].astype(BF16))
    h = _ple(h, p[0].reshape(n, -1), ln_ple, w_ple_gate[0].astype(BF16),
             w_ple_proj[0].astype(BF16), ln_final[None])
    return h.reshape(bsz, t, d)
```

```python
import functools

import numpy as np
import jax
import jax.numpy as jnp
from jax import lax
from jax.experimental import pallas as pl
from jax.experimental.pallas import tpu as pltpu

F32 = jnp.float32
BF16 = jnp.bfloat16
I32 = jnp.int32

D_MODEL = 1024
HEAD_DIM = 64
A_HEADS = 8
A_KV = 2
BLOCK = 128
B_HEADS = 8
IDX_HEADS = 8
TOPK = 256
D_FF = 2816
D_PLE = 256
EPS = 1e-6
NEG = -1e30

LANES = 128
VMEM_LIMIT = 48 * 1024 * 1024

_SLOPES = [float(np.float32(2.0) ** np.float32(-8.0 * i / (A_HEADS + B_HEADS)))
           for i in range(1, A_HEADS + B_HEADS + 1)]
SLOPES_A = _SLOPES[:A_HEADS]
SLOPES_B = _SLOPES[A_HEADS:]
LOG2E = float(np.log2(np.e))


def _bf16_terms(x, n=3):
    out = []
    for _ in range(n):
        t = float(np.float32(x).astype(jnp.bfloat16))
        out.append(t)
        x = x - t
    return out


SLOPE2_TERMS = [_bf16_terms(s * LOG2E) for s in SLOPES_B]


def _rms(x, g):
    return x * lax.rsqrt(jnp.mean(x * x, axis=-1, keepdims=True) + EPS) * g


def _cparams(*sem):
    return pltpu.CompilerParams(dimension_semantics=sem, vmem_limit_bytes=VMEM_LIMIT)


FFN_TM = 512
FFN_TF = 1408


def _ffn_kernel(h_ref, ln_ref, wa_ref, wb_ref, wo_ref, o_ref, xn_ref, acc_ref):
    j = pl.program_id(1)

    @pl.when(j == 0)
    def _():
        xn_ref[...] = _rms(h_ref[...], ln_ref[...]).astype(BF16)
        acc_ref[...] = jnp.zeros_like(acc_ref)

    xn = xn_ref[...]
    a = jnp.dot(xn, wa_ref[...], preferred_element_type=F32)
    b = jnp.dot(xn, wb_ref[...], preferred_element_type=F32)
    g = (a * jax.nn.sigmoid(a) * b).astype(BF16)
    acc_ref[...] += jnp.dot(g, wo_ref[...], preferred_element_type=F32)

    @pl.when(j == pl.num_programs(1) - 1)
    def _():
        o_ref[...] = h_ref[...] + 0.5 * acc_ref[...]


def _ffn(h, ln, w_in, w_out):
    n = h.shape[0]
    nj = D_FF // FFN_TF
    return pl.pallas_call(
        _ffn_kernel,
        grid=(n // FFN_TM, nj),
        in_specs=[
            pl.BlockSpec((FFN_TM, D_MODEL), lambda i, j: (i, 0)),
            pl.BlockSpec((1, D_MODEL), lambda i, j: (0, 0)),
            pl.BlockSpec((D_MODEL, FFN_TF), lambda i, j: (0, j)),
            pl.BlockSpec((D_MODEL, FFN_TF), lambda i, j: (0, j + nj)),
            pl.BlockSpec((FFN_TF, D_MODEL), lambda i, j: (j, 0)),
        ],
        out_specs=pl.BlockSpec((FFN_TM, D_MODEL), lambda i, j: (i, 0)),
        out_shape=jax.ShapeDtypeStruct((n, D_MODEL), F32),
        scratch_shapes=[pltpu.VMEM((FFN_TM, D_MODEL), BF16), pltpu.VMEM((FFN_TM, D_MODEL), F32)],
        compiler_params=_cparams("parallel", "arbitrary"),
        name="ffn",
    )(h, ln, w_in, w_in, w_out)


PROJ_TM = 256


def _inproj_kernel(h_ref, ln_ref, waq, wka, wav, wiq, wbq, wki, wkb, wv, wiw, wg,
                   o_aqt, o_ka, o_avt, o_iqt, o_bqt, o_ki, o_kb, o_vt, o_iw, o_g):
    u = _rms(h_ref[...], ln_ref[...]).astype(BF16)

    def mm(w):
        return jnp.dot(u, w[...], preferred_element_type=F32)

    o_ka[...] = mm(wka).astype(BF16)
    o_iw[...] = mm(wiw)
    o_g[...] = jax.nn.sigmoid(mm(wg))

    row = lax.broadcasted_iota(I32, (PROJ_TM, LANES), 0)
    lane = lax.broadcasted_iota(I32, (PROJ_TM, LANES), 1)
    rel = jnp.where((lane >= HEAD_DIM) & (lane < HEAD_DIM + 3), (row % BLOCK).astype(F32), 0.0)
    o_ki[...] = mm(wki).astype(BF16)
    o_kb[...] = (mm(wkb) + rel).astype(BF16)
    v1 = jnp.where(lane == HEAD_DIM, 1.0, mm(wv))
    aq = mm(waq)
    av = mm(wav)
    iq = mm(wiq)
    bq = mm(wbq)
    row64 = lax.broadcasted_iota(I32, (HEAD_DIM, BLOCK), 0)
    zeros64 = jnp.zeros((HEAD_DIM, BLOCK), F32)
    for blk in range(PROJ_TM // BLOCK):
        tok = slice(blk * BLOCK, (blk + 1) * BLOCK)
        o_vt[blk] = v1[tok].T.astype(BF16)
        o_avt[blk] = av[tok].T.astype(BF16)
        for j in range(B_HEADS // 2):
            aq_t = aq[tok, j * LANES:(j + 1) * LANES].T
            iq_t = iq[tok, j * LANES:(j + 1) * LANES].T
            bq_t = bq[tok, j * LANES:(j + 1) * LANES].T
            for hh in range(2):
                h = 2 * j + hh
                cols = slice(h * BLOCK, (h + 1) * BLOCK)
                part = slice(hh * HEAD_DIM, (hh + 1) * HEAD_DIM)
                o_aqt[blk, :, cols] = jnp.concatenate([aq_t[part], zeros64], axis=0).astype(BF16)
                o_iqt[blk, :, cols] = jnp.concatenate([iq_t[part], zeros64], axis=0).astype(BF16)
                slope_rows = zeros64
                for r, term in enumerate(SLOPE2_TERMS[h]):
                    slope_rows = jnp.where(row64 == r, term, slope_rows)
                o_bqt[blk, :, cols] = jnp.concatenate([bq_t[part], slope_rows], axis=0).astype(BF16)


def _split_in_proj(w):
    scale = HEAD_DIM ** -0.5
    aq, ak, av = w[:, 0:512] * scale, w[:, 512:640], w[:, 640:768]
    bq, bk, bv = w[:, 768:1280] * (scale * LOG2E), w[:, 1280:1344], w[:, 1344:1408]
    iq, ik, iw = w[:, 1408:1920] * scale, w[:, 1920:1984], w[:, 1984:1992]
    g = w[:, 1992:4040]
    z64 = jnp.zeros((w.shape[0], HEAD_DIM), w.dtype)
    return dict(
        waq=aq.astype(BF16),
        wka=jnp.concatenate([ak[:, :HEAD_DIM], z64, ak[:, HEAD_DIM:], z64], axis=1).astype(BF16),
        wav=av.astype(BF16),
        wiq=iq.astype(BF16),
        wbq=bq.astype(BF16),
        wki=jnp.concatenate([ik, z64], axis=1).astype(BF16),
        wkb=jnp.concatenate([bk, z64], axis=1).astype(BF16),
        wv=jnp.concatenate([bv, z64], axis=1).astype(BF16),
        wiw=jnp.pad(iw, ((0, 0), (0, LANES - IDX_HEADS))).astype(BF16),
        wg=g.astype(BF16),
    )


def _in_proj(h, ln, ws):
    n = h.shape[0]
    blocks = PROJ_TM // BLOCK
    names = ["waq", "wka", "wav", "wiq", "wbq", "wki", "wkb", "wv", "wiw", "wg"]
    rows = lambda wd, dt: (pl.BlockSpec((PROJ_TM, wd), lambda i: (i, 0)), jax.ShapeDtypeStruct((n, wd), dt))
    tiles = lambda r, c: (pl.BlockSpec((blocks, r, c), lambda i: (i, 0, 0)),
                          jax.ShapeDtypeStruct((n // BLOCK, r, c), BF16))
    outs = [tiles(LANES, A_HEADS * BLOCK), rows(A_KV * LANES, BF16), tiles(LANES, BLOCK),
            tiles(LANES, IDX_HEADS * BLOCK), tiles(LANES, B_HEADS * BLOCK),
            rows(LANES, BF16), rows(LANES, BF16), tiles(LANES, BLOCK),
            rows(LANES, F32), rows(2 * D_MODEL, F32)]
    return pl.pallas_call(
        _inproj_kernel,
        grid=(n // PROJ_TM,),
        in_specs=[pl.BlockSpec((PROJ_TM, D_MODEL), lambda i: (i, 0)),
                  pl.BlockSpec((1, D_MODEL), lambda i: (0, 0))]
        + [pl.BlockSpec((D_MODEL, ws[k].shape[1]), lambda i: (0, 0)) for k in names],
        out_specs=[o[0] for o in outs],
        out_shape=[o[1] for o in outs],
        compiler_params=_cparams("parallel"),
        name="in_proj",
    )(h, ln, *[ws[k] for k in names])


def _swa_kernel(sink_ref, qt_ref, kp_ref, kc_ref, vtp_ref, vtc_ref, o_ref):
    n = pl.program_id(1)
    qt = qt_ref[0, 0]
    j = lax.broadcasted_iota(I32, (BLOCK, BLOCK), 0)
    i = lax.broadcasted_iota(I32, (BLOCK, BLOCK), 1)
    lower = j <= i
    dcur = (i - j).astype(F32)
    dprev = (i - j + BLOCK).astype(F32)
    prev_bias = jnp.where(n > 0, 0.0, NEG).astype(F32)
    per_group = A_HEADS // A_KV
    outs = []
    for g in range(A_KV):
        k2 = jnp.concatenate([kp_ref[0, :, g * LANES:(g + 1) * LANES],
                              kc_ref[0, :, g * LANES:(g + 1) * LANES]], axis=0)
        s2 = jnp.dot(k2, qt[:, g * per_group * BLOCK:(g + 1) * per_group * BLOCK],
                     preferred_element_type=F32)
        ps = []
        for r in range(per_group):
            h = g * per_group + r
            sp = s2[0:BLOCK, r * BLOCK:(r + 1) * BLOCK]
            sc = s2[BLOCK:2 * BLOCK, r * BLOCK:(r + 1) * BLOCK]
            s = jnp.where(lower, sc - SLOPES_A[h] * dcur, sp - SLOPES_A[h] * dprev + prev_bias)
            sink = sink_ref[h]
            m = jnp.maximum(jnp.max(s, axis=0, keepdims=True), sink)
            e = jnp.exp(s - m)
            p = e / (jnp.sum(e, axis=0, keepdims=True) + jnp.exp(sink - m))
            ps.append(jnp.concatenate([jnp.where(lower, 0.0, p), jnp.where(lower, p, 0.0)],
                                      axis=0).astype(BF16))
        rows = slice(g * HEAD_DIM, (g + 1) * HEAD_DIM)
        vt2 = jnp.concatenate([vtp_ref[0, 0, rows, :], vtc_ref[0, 0, rows, :]], axis=1)
        o_t = jnp.dot(vt2, jnp.concatenate(ps, axis=1), preferred_element_type=F32)
        for r in range(0, per_group, 2):
            pair = jnp.concatenate([o_t[:, r * BLOCK:(r + 1) * BLOCK],
                                    o_t[:, (r + 1) * BLOCK:(r + 2) * BLOCK]], axis=0)
            outs.append(pair.T)
    o_ref[0] = jnp.concatenate(outs, axis=1).astype(BF16)


def _swa(sink, aqt, ka, avt):
    bsz, nb = aqt.shape[:2]
    t = nb * BLOCK
    prev = lambda b, n: (b, jnp.maximum(n - 1, 0), 0)
    prev4 = lambda b, n: (b, jnp.maximum(n - 1, 0), 0, 0)
    return pl.pallas_call(
        _swa_kernel,
        grid=(bsz, nb),
        in_specs=[
            pl.BlockSpec(memory_space=pltpu.SMEM),
            pl.BlockSpec((1, 1, LANES, A_HEADS * BLOCK), lambda b, n: (b, n, 0, 0)),
            pl.BlockSpec((1, BLOCK, A_KV * LANES), prev),
            pl.BlockSpec((1, BLOCK, A_KV * LANES), lambda b, n: (b, n, 0)),
            pl.BlockSpec((1, 1, LANES, BLOCK), prev4),
            pl.BlockSpec((1, 1, LANES, BLOCK), lambda b, n: (b, n, 0, 0)),
        ],
        out_specs=pl.BlockSpec((1, BLOCK, A_HEADS * HEAD_DIM), lambda b, n: (b, n, 0)),
        out_shape=jax.ShapeDtypeStruct((bsz, t, A_HEADS * HEAD_DIM), BF16),
        compiler_params=_cparams("parallel", "arbitrary"),
        name="swa",
    )(sink, aqt, ka, ka, avt, avt)


CK = 256
SUB = 128
PIPE = 4
ACC_ROWS = 80
TINY = 1.1754944e-38


def _tree_sum(xs):
    while len(xs) > 1:
        xs = [a + b for a, b in zip(xs[0::2], xs[1::2])]
    return xs[0]


def _for_chunks(nck, unroll, body, carry):
    def group(i, carry):
        for u in range(unroll):
            carry = body(i * unroll + u, carry)
        return carry
    nfull = nck // unroll
    carry = lax.fori_loop(0, nfull, group, carry)
    return lax.fori_loop(nfull * unroll, nck, body, carry)


def _dsa_kernel(iqt_ref, iw_ref, bqt_ref, ki_ref, kb_ref, vt_ref, yb_ref, sc_ref, thr_ref, s_ref, p_ref,
                alpha_ref, acc_ref):
    n = pl.program_id(1)
    nck = n // (CK // BLOCK) + 1
    tq = n * BLOCK + lax.broadcasted_iota(I32, (CK, LANES), 1)
    rel = lax.broadcasted_iota(I32, (CK, LANES), 0)

    qi_t = iqt_ref[0, 0]
    iw_t = (iw_ref[0] * (IDX_HEADS ** -0.5)).T

    def score_chunk(c, carry):
        smin, smax = carry
        start = pl.multiple_of(c * CK, CK)
        s = jnp.dot(ki_ref[0, pl.ds(start, CK), :], qi_t, preferred_element_type=F32)
        acc = _tree_sum([jnp.maximum(s[:, h * LANES:(h + 1) * LANES], 0.0) * iw_t[h:h + 1, :]
                         for h in range(IDX_HEADS)])
        causal = c * CK + rel <= tq
        sc_ref[c] = jnp.where(causal, acc, NEG)
        smax = jnp.maximum(smax, jnp.max(jnp.where(causal, acc, NEG), axis=0, keepdims=True))
        smin = jnp.minimum(smin, jnp.min(jnp.where(causal, acc, -NEG), axis=0, keepdims=True))
        return smin, smax

    smin, smax = _for_chunks(nck, 4, score_chunk,
                             (jnp.full((1, LANES), -NEG, F32), jnp.full((1, LANES), NEG, F32)))

    thr_ref[...] = jnp.full(thr_ref.shape, NEG, F32)

    @pl.when(n >= 2)
    def _():
        def count_ge(probe):
            def body(c, cnt):
                hit = jnp.where(sc_ref[c] >= probe, 1.0, 0.0)
                return cnt + jnp.sum(hit.reshape(CK // 32, 4, 8, LANES), axis=0)
            cnt = _for_chunks(nck, 4, body, jnp.zeros((4, 8, LANES), F32))
            return jnp.sum(cnt.reshape(32, LANES), axis=0, keepdims=True)

        def probe_of(lo, hi):
            p = 0.5 * lo + 0.5 * hi
            p = jnp.where((lo < 0.0) & (hi > 0.0), 0.0, p)
            p = jnp.where((lo == 0.0) & (hi > TINY), TINY, p)
            return jnp.where((hi == 0.0) & (lo < -TINY), -TINY, p)

        def cond(st):
            lo, hi, c_lo, c_hi = st
            p = probe_of(lo, hi)
            done = (c_lo == TOPK) | (hi - lo <= TINY) | ~((p > lo) & (p < hi))
            return jnp.max(jnp.where(done, 0.0, 1.0)) > 0.0

        def step(st):
            lo, hi, c_lo, c_hi = st
            p = probe_of(lo, hi)
            c = count_ge(p)
            ge = c >= TOPK
            return (jnp.where(ge, p, lo), jnp.where(ge, hi, p),
                    jnp.where(ge, c, c_lo), jnp.where(ge, c_hi, c))

        n_causal = (tq[0:1, :] + 1).astype(F32)
        st0 = (smin, smax + (jnp.abs(smax) * 1e-6 + TINY), n_causal, jnp.zeros((1, LANES), F32))
        lo, hi, c_lo, c_hi = lax.while_loop(cond, lambda st: step(step(st)), st0)
        thr_ref[...] = jnp.broadcast_to(lo, thr_ref.shape)

        tied = c_lo > TOPK

        @pl.when(jnp.max(jnp.where(tied, 1.0, 0.0)) > 0.0)
        def _():
            keep = TOPK - c_hi
            tri = jnp.where(lax.broadcasted_iota(I32, (CK, CK), 0) >= lax.broadcasted_iota(I32, (CK, CK), 1),
                            1.0, 0.0).astype(BF16)

            def drop(c, before):
                x = sc_ref[c]
                eq = x == lo
                prefix = before + jnp.dot(tri, jnp.where(eq, 1.0, 0.0).astype(BF16),
                                          preferred_element_type=F32)
                sc_ref[c] = jnp.where(eq & tied & (prefix > keep), NEG, x)
                return prefix[CK - 1:CK, :]

            lax.fori_loop(0, nck, drop, jnp.zeros((1, LANES), F32))

    thr = thr_ref[0:1, :]
    q2_t = bqt_ref[0, 0]
    rel_s = lax.broadcasted_iota(I32, (SUB, LANES), 0)
    tq_s = n * BLOCK + lax.broadcasted_iota(I32, (SUB, LANES), 1)
    slope2 = [float(np.float32(sum(t))) for t in SLOPE2_TERMS]

    acc_ref[...] = jnp.zeros_like(acc_ref)

    last = pl.num_programs(1) - 1

    def stage_logits(t, k):
        start = pl.multiple_of(jnp.minimum(t, last) * SUB, SUB)
        s_ref[k] = jnp.dot(kb_ref[0, pl.ds(start, SUB), :], q2_t, preferred_element_type=F32)

    def softmax(t, k, m):
        tc = jnp.minimum(t, n)
        rows = pl.ds(pl.multiple_of((tc % 2) * SUB, SUB), SUB)
        sel = (sc_ref[tc // 2, rows, :] >= thr) & (t * SUB + rel_s <= tq_s)
        negb = jnp.where(sel, 0.0, NEG)
        base = (jnp.zeros((1, LANES), I32) + t * SUB).astype(F32)
        m_out = []
        for h in range(B_HEADS):
            l = s_ref[k, :, h * LANES:(h + 1) * LANES] + negb
            off = base * slope2[h]
            m_new = jnp.maximum(m[h], jnp.max(l, axis=0, keepdims=True) + off)
            alpha_ref[k, h:h + 1, :] = jnp.exp2(m[h] - m_new)
            p_ref[k, :, h * LANES:(h + 1) * LANES] = jnp.exp2(l - (m_new - off)).astype(BF16)
            m_out.append(m_new)
        return m_out

    def pv_accumulate(t, k):
        v_t = vt_ref[0, jnp.clip(t, 0, last), 0:ACC_ROWS, :]
        pv = jnp.dot(v_t, p_ref[k], preferred_element_type=F32)
        alpha = jnp.concatenate([alpha_ref[k, h:h + 1, :] for h in range(B_HEADS)], axis=1)
        acc_ref[...] = acc_ref[...] * alpha + pv

    def group(g, m):
        for k in range(PIPE):
            pv_accumulate((g - 1) * PIPE + k, k)
        for k in range(PIPE):
            m = softmax(g * PIPE + k, k, m)
        for k in range(PIPE):
            stage_logits((g + 1) * PIPE + k, k)
        return m

    ngroups = n // PIPE + 1
    for k in range(PIPE):
        stage_logits(k, k)
    p_ref[...] = jnp.zeros(p_ref.shape, BF16)
    alpha_ref[...] = jnp.ones(alpha_ref.shape, F32)
    lax.fori_loop(0, ngroups, group, [jnp.full((1, LANES), NEG, F32) for _ in range(B_HEADS)])
    for k in range(PIPE):
        pv_accumulate((ngroups - 1) * PIPE + k, k)

    outs = []
    pad = jnp.zeros((LANES - HEAD_DIM, LANES), F32)
    for h in range(B_HEADS):
        blk = acc_ref[:, h * LANES:(h + 1) * LANES]
        o_t = blk[0:HEAD_DIM] / blk[HEAD_DIM:HEAD_DIM + 1]
        outs.append(jnp.concatenate([o_t, pad], axis=0).T[:, 0:HEAD_DIM])
    yb_ref[0] = jnp.concatenate(outs, axis=1).astype(BF16)


def _dsa(iqt, iw, bqt, ki, kb, vt):
    bsz, t, _ = iw.shape
    nb = t // BLOCK
    return pl.pallas_call(
        _dsa_kernel,
        grid=(bsz, nb),
        in_specs=[
            pl.BlockSpec((1, 1, LANES, IDX_HEADS * BLOCK), lambda b, n: (b, n, 0, 0)),
            pl.BlockSpec((1, BLOCK, LANES), lambda b, n: (b, n, 0)),
            pl.BlockSpec((1, 1, LANES, B_HEADS * BLOCK), lambda b, n: (b, n, 0, 0)),
            pl.BlockSpec((1, t, LANES), lambda b, n: (b, 0, 0)),
            pl.BlockSpec((1, t, LANES), lambda b, n: (b, 0, 0)),
            pl.BlockSpec((1, nb, LANES, BLOCK), lambda b, n: (b, 0, 0, 0)),
        ],
        out_specs=pl.BlockSpec((1, BLOCK, B_HEADS * HEAD_DIM), lambda b, n: (b, n, 0)),
        out_shape=jax.ShapeDtypeStruct((bsz, t, B_HEADS * HEAD_DIM), BF16),
        scratch_shapes=[
            pltpu.VMEM((t // CK, CK, LANES), F32),
            pltpu.VMEM((8, LANES), F32),
            pltpu.VMEM((PIPE, SUB, B_HEADS * LANES), F32),
            pltpu.VMEM((PIPE, SUB, B_HEADS * LANES), BF16),
            pltpu.VMEM((PIPE, B_HEADS, LANES), F32),
            pltpu.VMEM((ACC_ROWS, B_HEADS * LANES), F32),
        ],
        compiler_params=_cparams("parallel", "arbitrary"),
        name="dsa",
    )(iqt, iw, bqt, ki, kb, vt)


MERGE_TM = 512


def _merge_kernel(h_ref, ya_ref, yb_ref, g_ref, wa_ref, wb_ref, wo_ref, o_ref):
    ta = jnp.dot(ya_ref[...], wa_ref[...], preferred_element_type=F32)
    tb = jnp.dot(yb_ref[...], wb_ref[...], preferred_element_type=F32)
    g = g_ref[...]
    mix = (g[:, :D_MODEL] * ta + g[:, D_MODEL:] * tb).astype(BF16)
    o_ref[...] = h_ref[...] + jnp.dot(mix, wo_ref[...], preferred_element_type=F32)


def _merge(h, ya, yb, g, wa, wb, wo):
    n = h.shape[0]
    row = lambda w: pl.BlockSpec((MERGE_TM, w), lambda i: (i, 0))
    full = lambda a: pl.BlockSpec(a.shape, lambda i: (0, 0))
    return pl.pallas_call(
        _merge_kernel,
        grid=(n // MERGE_TM,),
        in_specs=[row(D_MODEL), row(ya.shape[1]), row(yb.shape[1]), row(2 * D_MODEL),
                  full(wa), full(wb), full(wo)],
        out_specs=row(D_MODEL),
        out_shape=jax.ShapeDtypeStruct((n, D_MODEL), F32),
        compiler_params=_cparams("parallel"),
        name="merge",
    )(h, ya, yb, g, wa, wb, wo)


def _ple_kernel(h_ref, p_ref, lnp_ref, wg_ref, wp_ref, lnf_ref, o_ref):
    x = h_ref[...]
    u = _rms(x, lnp_ref[...]).astype(BF16)
    gate = jax.nn.sigmoid(jnp.dot(u, wg_ref[...], preferred_element_type=F32))
    proj = jnp.dot(p_ref[...].astype(BF16), wp_ref[...], preferred_element_type=F32)
    o_ref[...] = _rms(x + gate * proj, lnf_ref[...])


def _ple(h, p, lnp, wg, wp, lnf):
    n = h.shape[0]
    row = lambda w: pl.BlockSpec((MERGE_TM, w), lambda i: (i, 0))
    full = lambda a: pl.BlockSpec(a.shape, lambda i: (0, 0))
    return pl.pallas_call(
        _ple_kernel,
        grid=(n // MERGE_TM,),
        in_specs=[row(D_MODEL), row(D_PLE), full(lnp), full(wg), full(wp), full(lnf)],
        out_specs=row(D_MODEL),
        out_shape=jax.ShapeDtypeStruct((n, D_MODEL), F32),
        compiler_params=_cparams("parallel"),
        name="ple",
    )(h, p, lnp, wg, wp, lnf)


def kernel(x, p, ln_ffn1, w_ffn1_in, w_ffn1_out, ln_mix, w_in, a_sink, w_br_a, w_br_b, w_out,
           ln_ffn2, w_ffn2_in, w_ffn2_out, ln_ple, w_ple_gate, w_ple_proj, ln_final):
    bsz, t, d = x.shape
    assert p.shape[0] == 1, "the final norm is fused into the single layer's last kernel"
    n = bsz * t
    h = x.reshape(n, d)
    h = _ffn(h, ln_ffn1, w_ffn1_in[0].astype(BF16), w_ffn1_out[0].astype(BF16))
    aqt, ka, avt, iqt, bqt, ki, kb, vt, iw, g = _in_proj(h, ln_mix, _split_in_proj(w_in[0]))
    tok = lambda a: a.reshape(bsz, t, a.shape[-1])
    blk = lambda a: a.reshape(bsz, t // BLOCK, a.shape[-2], a.shape[-1])
    ya = _swa(a_sink[0], blk(aqt), tok(ka), blk(avt))
    yb = _dsa(blk(iqt), tok(iw), blk(bqt), tok(ki), tok(kb), blk(vt))
    h = _merge(h, ya.reshape(n, -1), yb.reshape(n, -1), g,
               w_br_a[0].astype(BF16), w_br_b[0].astype(BF16), w_out[0].astype(BF16))
    h = _ffn(h, ln_ffn2, w_ffn2_in[0].astype(BF16), w_ffn2_out[0].astype(BF16))
    h = _ple(h, p[0].reshape(n, -1), ln_ple, w_ple_gate[0].astype(BF16),
             w_ple_proj[0].astype(BF16), ln_final[None])
    return h.reshape(bsz, t, d)
```

```python
import functools

import numpy as np
import jax
import jax.numpy as jnp
from jax import lax
from jax.experimental import pallas as pl
from jax.experimental.pallas import tpu as pltpu

F32 = jnp.float32
BF16 = jnp.bfloat16
I32 = jnp.int32

D_MODEL = 1024
HEAD_DIM = 64
A_HEADS = 8
A_KV = 2
BLOCK = 128
B_HEADS = 8
IDX_HEADS = 8
TOPK = 256
D_FF = 2816
D_PLE = 256
EPS = 1e-6
NEG = -1e30

LANES = 128
VMEM_LIMIT = 48 * 1024 * 1024

_SLOPES = [float(np.float32(2.0) ** np.float32(-8.0 * i / (A_HEADS + B_HEADS)))
           for i in range(1, A_HEADS + B_HEADS + 1)]
SLOPES_A = _SLOPES[:A_HEADS]
SLOPES_B = _SLOPES[A_HEADS:]
LOG2E = float(np.log2(np.e))


def _bf16_terms(x, n=3):
    out = []
    for _ in range(n):
        t = float(np.float32(x).astype(jnp.bfloat16))
        out.append(t)
        x = x - t
    return out


SLOPE2_TERMS = [_bf16_terms(s * LOG2E) for s in SLOPES_B]


def _rms(x, g):
    return x * lax.rsqrt(jnp.mean(x * x, axis=-1, keepdims=True) + EPS) * g


def _cparams(*sem):
    return pltpu.CompilerParams(dimension_semantics=sem, vmem_limit_bytes=VMEM_LIMIT)


FFN_TM = 512


def _ffn_kernel(h_ref, ln_ref, wa_ref, wb_ref, wo_ref, o_ref):
    x = h_ref[...]
    xn = _rms(x, ln_ref[...]).astype(BF16)
    a = jnp.dot(xn, wa_ref[...], preferred_element_type=F32)
    b = jnp.dot(xn, wb_ref[...], preferred_element_type=F32)
    g = (a * jax.nn.sigmoid(a) * b).astype(BF16)
    o_ref[...] = x + 0.5 * jnp.dot(g, wo_ref[...], preferred_element_type=F32)


def _ffn(h, ln, w_in, w_out):
    n = h.shape[0]
    once = pl.Buffered(1)
    return pl.pallas_call(
        _ffn_kernel,
        grid=(n // FFN_TM,),
        in_specs=[
            pl.BlockSpec((FFN_TM, D_MODEL), lambda i: (i, 0)),
            pl.BlockSpec((1, D_MODEL), lambda i: (0, 0)),
            pl.BlockSpec((D_MODEL, D_FF), lambda i: (0, 0), pipeline_mode=once),
            pl.BlockSpec((D_MODEL, D_FF), lambda i: (0, 1), pipeline_mode=once),
            pl.BlockSpec((D_FF, D_MODEL), lambda i: (0, 0), pipeline_mode=once),
        ],
        out_specs=pl.BlockSpec((FFN_TM, D_MODEL), lambda i: (i, 0)),
        out_shape=jax.ShapeDtypeStruct((n, D_MODEL), F32),
        compiler_params=_cparams("parallel"),
        name="ffn",
    )(h, ln, w_in, w_in, w_out)


PROJ_TM = 256


def _inproj_kernel(h_ref, ln_ref, waq, wka, wav, wiq, wbq, wki, wkb, wv, wiw, wg,
                   o_aqt, o_ka, o_avt, o_iqt, o_bqt, o_ki, o_kb, o_vt, o_iw, o_g):
    u = _rms(h_ref[...], ln_ref[...]).astype(BF16)

    def mm(w):
        return jnp.dot(u, w[...], preferred_element_type=F32)

    o_ka[...] = mm(wka).astype(BF16)
    o_iw[...] = mm(wiw)
    o_g[...] = jax.nn.sigmoid(mm(wg))

    row = lax.broadcasted_iota(I32, (PROJ_TM, LANES), 0)
    lane = lax.broadcasted_iota(I32, (PROJ_TM, LANES), 1)
    rel = jnp.where((lane >= HEAD_DIM) & (lane < HEAD_DIM + 3), (row % BLOCK).astype(F32), 0.0)
    o_ki[...] = mm(wki).astype(BF16)
    o_kb[...] = (mm(wkb) + rel).astype(BF16)
    v1 = jnp.where(lane == HEAD_DIM, 1.0, mm(wv))
    aq = mm(waq)
    av = mm(wav)
    iq = mm(wiq)
    bq = mm(wbq)
    row64 = lax.broadcasted_iota(I32, (HEAD_DIM, BLOCK), 0)
    zeros64 = jnp.zeros((HEAD_DIM, BLOCK), F32)
    for blk in range(PROJ_TM // BLOCK):
        tok = slice(blk * BLOCK, (blk + 1) * BLOCK)
        o_vt[blk] = v1[tok].T.astype(BF16)
        o_avt[blk] = av[tok].T.astype(BF16)
        for j in range(B_HEADS // 2):
            aq_t = aq[tok, j * LANES:(j + 1) * LANES].T
            iq_t = iq[tok, j * LANES:(j + 1) * LANES].T
            bq_t = bq[tok, j * LANES:(j + 1) * LANES].T
            for hh in range(2):
                h = 2 * j + hh
                cols = slice(h * BLOCK, (h + 1) * BLOCK)
                part = slice(hh * HEAD_DIM, (hh + 1) * HEAD_DIM)
                o_aqt[blk, :, cols] = jnp.concatenate([aq_t[part], zeros64], axis=0).astype(BF16)
                o_iqt[blk, :, cols] = jnp.concatenate([iq_t[part], zeros64], axis=0).astype(BF16)
                slope_rows = zeros64
                for r, term in enumerate(SLOPE2_TERMS[h]):
                    slope_rows = jnp.where(row64 == r, term, slope_rows)
                o_bqt[blk, :, cols] = jnp.concatenate([bq_t[part], slope_rows], axis=0).astype(BF16)


def _split_in_proj(w):
    scale = HEAD_DIM ** -0.5
    aq, ak, av = w[:, 0:512] * scale, w[:, 512:640], w[:, 640:768]
    bq, bk, bv = w[:, 768:1280] * (scale * LOG2E), w[:, 1280:1344], w[:, 1344:1408]
    iq, ik, iw = w[:, 1408:1920] * scale, w[:, 1920:1984], w[:, 1984:1992]
    g = w[:, 1992:4040]
    z64 = jnp.zeros((w.shape[0], HEAD_DIM), w.dtype)
    return dict(
        waq=aq.astype(BF16),
        wka=jnp.concatenate([ak[:, :HEAD_DIM], z64, ak[:, HEAD_DIM:], z64], axis=1).astype(BF16),
        wav=av.astype(BF16),
        wiq=iq.astype(BF16),
        wbq=bq.astype(BF16),
        wki=jnp.concatenate([ik, z64], axis=1).astype(BF16),
        wkb=jnp.concatenate([bk, z64], axis=1).astype(BF16),
        wv=jnp.concatenate([bv, z64], axis=1).astype(BF16),
        wiw=jnp.pad(iw, ((0, 0), (0, LANES - IDX_HEADS))).astype(BF16),
        wg=g.astype(BF16),
    )


def _in_proj(h, ln, ws):
    n = h.shape[0]
    blocks = PROJ_TM // BLOCK
    names = ["waq", "wka", "wav", "wiq", "wbq", "wki", "wkb", "wv", "wiw", "wg"]
    rows = lambda wd, dt: (pl.BlockSpec((PROJ_TM, wd), lambda i: (i, 0)), jax.ShapeDtypeStruct((n, wd), dt))
    tiles = lambda r, c: (pl.BlockSpec((blocks, r, c), lambda i: (i, 0, 0)),
                          jax.ShapeDtypeStruct((n // BLOCK, r, c), BF16))
    outs = [tiles(LANES, A_HEADS * BLOCK), rows(A_KV * LANES, BF16), tiles(LANES, BLOCK),
            tiles(LANES, IDX_HEADS * BLOCK), tiles(LANES, B_HEADS * BLOCK),
            rows(LANES, BF16), rows(LANES, BF16), tiles(LANES, BLOCK),
            rows(LANES, F32), rows(2 * D_MODEL, F32)]
    return pl.pallas_call(
        _inproj_kernel,
        grid=(n // PROJ_TM,),
        in_specs=[pl.BlockSpec((PROJ_TM, D_MODEL), lambda i: (i, 0)),
                  pl.BlockSpec((1, D_MODEL), lambda i: (0, 0))]
        + [pl.BlockSpec((D_MODEL, ws[k].shape[1]), lambda i: (0, 0)) for k in names],
        out_specs=[o[0] for o in outs],
        out_shape=[o[1] for o in outs],
        compiler_params=_cparams("parallel"),
        name="in_proj",
    )(h, ln, *[ws[k] for k in names])


def _swa_kernel(sink_ref, qt_ref, kp_ref, kc_ref, vtp_ref, vtc_ref, o_ref):
    n = pl.program_id(1)
    qt = qt_ref[0, 0]
    j = lax.broadcasted_iota(I32, (BLOCK, BLOCK), 0)
    i = lax.broadcasted_iota(I32, (BLOCK, BLOCK), 1)
    lower = j <= i
    dcur = (i - j).astype(F32)
    dprev = (i - j + BLOCK).astype(F32)
    prev_bias = jnp.where(n > 0, 0.0, NEG).astype(F32)
    per_group = A_HEADS // A_KV
    outs = []
    for g in range(A_KV):
        k2 = jnp.concatenate([kp_ref[0, :, g * LANES:(g + 1) * LANES],
                              kc_ref[0, :, g * LANES:(g + 1) * LANES]], axis=0)
        s2 = jnp.dot(k2, qt[:, g * per_group * BLOCK:(g + 1) * per_group * BLOCK],
                     preferred_element_type=F32)
        ps = []
        for r in range(per_group):
            h = g * per_group + r
            sp = s2[0:BLOCK, r * BLOCK:(r + 1) * BLOCK]
            sc = s2[BLOCK:2 * BLOCK, r * BLOCK:(r + 1) * BLOCK]
            s = jnp.where(lower, sc - SLOPES_A[h] * dcur, sp - SLOPES_A[h] * dprev + prev_bias)
            sink = sink_ref[h]
            m = jnp.maximum(jnp.max(s, axis=0, keepdims=True), sink)
            e = jnp.exp(s - m)
            p = e / (jnp.sum(e, axis=0, keepdims=True) + jnp.exp(sink - m))
            ps.append(jnp.concatenate([jnp.where(lower, 0.0, p), jnp.where(lower, p, 0.0)],
                                      axis=0).astype(BF16))
        rows = slice(g * HEAD_DIM, (g + 1) * HEAD_DIM)
        vt2 = jnp.concatenate([vtp_ref[0, 0, rows, :], vtc_ref[0, 0, rows, :]], axis=1)
        o_t = jnp.dot(vt2, jnp.concatenate(ps, axis=1), preferred_element_type=F32)
        for r in range(0, per_group, 2):
            pair = jnp.concatenate([o_t[:, r * BLOCK:(r + 1) * BLOCK],
                                    o_t[:, (r + 1) * BLOCK:(r + 2) * BLOCK]], axis=0)
            outs.append(pair.T)
    o_ref[0] = jnp.concatenate(outs, axis=1).astype(BF16)


def _swa(sink, aqt, ka, avt):
    bsz, nb = aqt.shape[:2]
    t = nb * BLOCK
    prev = lambda b, n: (b, jnp.maximum(n - 1, 0), 0)
    prev4 = lambda b, n: (b, jnp.maximum(n - 1, 0), 0, 0)
    return pl.pallas_call(
        _swa_kernel,
        grid=(bsz, nb),
        in_specs=[
            pl.BlockSpec(memory_space=pltpu.SMEM),
            pl.BlockSpec((1, 1, LANES, A_HEADS * BLOCK), lambda b, n: (b, n, 0, 0)),
            pl.BlockSpec((1, BLOCK, A_KV * LANES), prev),
            pl.BlockSpec((1, BLOCK, A_KV * LANES), lambda b, n: (b, n, 0)),
            pl.BlockSpec((1, 1, LANES, BLOCK), prev4),
            pl.BlockSpec((1, 1, LANES, BLOCK), lambda b, n: (b, n, 0, 0)),
        ],
        out_specs=pl.BlockSpec((1, BLOCK, A_HEADS * HEAD_DIM), lambda b, n: (b, n, 0)),
        out_shape=jax.ShapeDtypeStruct((bsz, t, A_HEADS * HEAD_DIM), BF16),
        compiler_params=_cparams("parallel", "arbitrary"),
        name="swa",
    )(sink, aqt, ka, ka, avt, avt)


CK = 256
SUB = 128
PIPE = 4
ACC_ROWS = 80
TINY = 1.1754944e-38
FREE_STEPS = 10


def _tree_sum(xs):
    while len(xs) > 1:
        xs = [a + b for a, b in zip(xs[0::2], xs[1::2])]
    return xs[0]


def _for_chunks(nck, unroll, body, carry):
    def group(i, carry):
        for u in range(unroll):
            carry = body(i * unroll + u, carry)
        return carry
    nfull = nck // unroll
    carry = lax.fori_loop(0, nfull, group, carry)
    return lax.fori_loop(nfull * unroll, nck, body, carry)


def _dsa_kernel(iqt_ref, iw_ref, bqt_ref, ki_ref, kb_ref, vt_ref, yb_ref, sc_ref, thr_ref, s_ref, p_ref,
                alpha_ref, acc_ref):
    n = pl.program_id(1)
    nck = n // (CK // BLOCK) + 1
    tq = n * BLOCK + lax.broadcasted_iota(I32, (CK, LANES), 1)
    rel = lax.broadcasted_iota(I32, (CK, LANES), 0)

    qi_t = iqt_ref[0, 0]
    iw_t = (iw_ref[0] * (IDX_HEADS ** -0.5)).T

    def score_chunk(c, carry):
        smin, smax = carry
        start = pl.multiple_of(c * CK, CK)
        s = jnp.dot(ki_ref[0, pl.ds(start, CK), :], qi_t, preferred_element_type=F32)
        acc = _tree_sum([jnp.maximum(s[:, h * LANES:(h + 1) * LANES], 0.0) * iw_t[h:h + 1, :]
                         for h in range(IDX_HEADS)])
        causal = c * CK + rel <= tq
        sc_ref[c] = jnp.where(causal, acc, NEG)
        smax = jnp.maximum(smax, jnp.max(jnp.where(causal, acc, NEG), axis=0, keepdims=True))
        smin = jnp.minimum(smin, jnp.min(jnp.where(causal, acc, -NEG), axis=0, keepdims=True))
        return smin, smax

    smin, smax = _for_chunks(nck, 4, score_chunk,
                             (jnp.full((1, LANES), -NEG, F32), jnp.full((1, LANES), NEG, F32)))

    thr_ref[...] = jnp.full(thr_ref.shape, NEG, F32)

    @pl.when(n >= 2)
    def _():
        def count_ge(probe):
            def body(c, cnt):
                hit = jnp.where(sc_ref[c] >= probe, 1.0, 0.0)
                return cnt + jnp.sum(hit.reshape(CK // 32, 4, 8, LANES), axis=0)
            cnt = _for_chunks(nck, 4, body, jnp.zeros((4, 8, LANES), F32))
            return jnp.sum(cnt.reshape(32, LANES), axis=0, keepdims=True)

        def probe_of(lo, hi):
            p = 0.5 * lo + 0.5 * hi
            p = jnp.where((lo < 0.0) & (hi > 0.0), 0.0, p)
            p = jnp.where((lo == 0.0) & (hi > TINY), TINY, p)
            return jnp.where((hi == 0.0) & (lo < -TINY), -TINY, p)

        def cond(st):
            lo, hi, c_lo, c_hi = st
            p = probe_of(lo, hi)
            done = (c_lo == TOPK) | (hi - lo <= TINY) | ~((p > lo) & (p < hi))
            return jnp.max(jnp.where(done, 0.0, 1.0)) > 0.0

        def step(st):
            lo, hi, c_lo, c_hi = st
            p = probe_of(lo, hi)
            c = count_ge(p)
            ge = c >= TOPK
            return (jnp.where(ge, p, lo), jnp.where(ge, hi, p),
                    jnp.where(ge, c, c_lo), jnp.where(ge, c_hi, c))

        n_causal = (tq[0:1, :] + 1).astype(F32)
        st0 = (smin, smax + (jnp.abs(smax) * 1e-6 + TINY), n_causal, jnp.zeros((1, LANES), F32))
        st1 = lax.fori_loop(0, FREE_STEPS, lambda _, st: step(st), st0)
        lo, hi, c_lo, c_hi = lax.while_loop(cond, lambda st: step(step(st)), st1)
        thr_ref[...] = jnp.broadcast_to(lo, thr_ref.shape)

        tied = c_lo > TOPK

        @pl.when(jnp.max(jnp.where(tied, 1.0, 0.0)) > 0.0)
        def _():
            keep = TOPK - c_hi
            tri = jnp.where(lax.broadcasted_iota(I32, (CK, CK), 0) >= lax.broadcasted_iota(I32, (CK, CK), 1),
                            1.0, 0.0).astype(BF16)

            def drop(c, before):
                x = sc_ref[c]
                eq = x == lo
                prefix = before + jnp.dot(tri, jnp.where(eq, 1.0, 0.0).astype(BF16),
                                          preferred_element_type=F32)
                sc_ref[c] = jnp.where(eq & tied & (prefix > keep), NEG, x)
                return prefix[CK - 1:CK, :]

            lax.fori_loop(0, nck, drop, jnp.zeros((1, LANES), F32))

    thr = thr_ref[0:1, :]
    q2_t = bqt_ref[0, 0]
    rel_s = lax.broadcasted_iota(I32, (SUB, LANES), 0)
    tq_s = n * BLOCK + lax.broadcasted_iota(I32, (SUB, LANES), 1)
    slope2 = [float(np.float32(sum(t))) for t in SLOPE2_TERMS]

    acc_ref[...] = jnp.zeros_like(acc_ref)

    last = pl.num_programs(1) - 1

    def stage_logits(t, k):
        start = pl.multiple_of(jnp.minimum(t, last) * SUB, SUB)
        s_ref[k] = jnp.dot(kb_ref[0, pl.ds(start, SUB), :], q2_t, preferred_element_type=F32)

    def softmax(t, k, m):
        tc = jnp.minimum(t, n)
        rows = pl.ds(pl.multiple_of((tc % 2) * SUB, SUB), SUB)
        sel = (sc_ref[tc // 2, rows, :] >= thr) & (t * SUB + rel_s <= tq_s)
        negb = jnp.where(sel, 0.0, NEG)
        base = (jnp.zeros((1, LANES), I32) + t * SUB).astype(F32)
        m_out = []
        for h in range(B_HEADS):
            l = s_ref[k, :, h * LANES:(h + 1) * LANES] + negb
            off = base * slope2[h]
            m_new = jnp.maximum(m[h], jnp.max(l, axis=0, keepdims=True) + off)
            alpha_ref[k, h:h + 1, :] = jnp.exp2(m[h] - m_new)
            p_ref[k, :, h * LANES:(h + 1) * LANES] = jnp.exp2(l - (m_new - off)).astype(BF16)
            m_out.append(m_new)
        return m_out

    def pv_accumulate(t, k):
        v_t = vt_ref[0, jnp.clip(t, 0, last), 0:ACC_ROWS, :]
        pv = jnp.dot(v_t, p_ref[k], preferred_element_type=F32)
        alpha = jnp.concatenate([alpha_ref[k, h:h + 1, :] for h in range(B_HEADS)], axis=1)
        acc_ref[...] = acc_ref[...] * alpha + pv

    def group(g, m):
        for k in range(PIPE):
            pv_accumulate((g - 1) * PIPE + k, k)
        for k in range(PIPE):
            m = softmax(g * PIPE + k, k, m)
        for k in range(PIPE):
            stage_logits((g + 1) * PIPE + k, k)
        return m

    ngroups = n // PIPE + 1
    for k in range(PIPE):
        stage_logits(k, k)
    p_ref[...] = jnp.zeros(p_ref.shape, BF16)
    alpha_ref[...] = jnp.ones(alpha_ref.shape, F32)
    lax.fori_loop(0, ngroups, group, [jnp.full((1, LANES), NEG, F32) for _ in range(B_HEADS)])
    for k in range(PIPE):
        pv_accumulate((ngroups - 1) * PIPE + k, k)

    outs = []
    pad = jnp.zeros((LANES - HEAD_DIM, LANES), F32)
    for h in range(B_HEADS):
        blk = acc_ref[:, h * LANES:(h + 1) * LANES]
        o_t = blk[0:HEAD_DIM] / blk[HEAD_DIM:HEAD_DIM + 1]
        outs.append(jnp.concatenate([o_t, pad], axis=0).T[:, 0:HEAD_DIM])
    yb_ref[0] = jnp.concatenate(outs, axis=1).astype(BF16)


def _dsa(iqt, iw, bqt, ki, kb, vt):
    bsz, t, _ = iw.shape
    nb = t // BLOCK
    return pl.pallas_call(
        _dsa_kernel,
        grid=(bsz, nb),
        in_specs=[
            pl.BlockSpec((1, 1, LANES, IDX_HEADS * BLOCK), lambda b, n: (b, n, 0, 0)),
            pl.BlockSpec((1, BLOCK, LANES), lambda b, n: (b, n, 0)),
            pl.BlockSpec((1, 1, LANES, B_HEADS * BLOCK), lambda b, n: (b, n, 0, 0)),
            pl.BlockSpec((1, t, LANES), lambda b, n: (b, 0, 0)),
            pl.BlockSpec((1, t, LANES), lambda b, n: (b, 0, 0)),
            pl.BlockSpec((1, nb, LANES, BLOCK), lambda b, n: (b, 0, 0, 0)),
        ],
        out_specs=pl.BlockSpec((1, BLOCK, B_HEADS * HEAD_DIM), lambda b, n: (b, n, 0)),
        out_shape=jax.ShapeDtypeStruct((bsz, t, B_HEADS * HEAD_DIM), BF16),
        scratch_shapes=[
            pltpu.VMEM((t // CK, CK, LANES), F32),
            pltpu.VMEM((8, LANES), F32),
            pltpu.VMEM((PIPE, SUB, B_HEADS * LANES), F32),
            pltpu.VMEM((PIPE, SUB, B_HEADS * LANES), BF16),
            pltpu.VMEM((PIPE, B_HEADS, LANES), F32),
            pltpu.VMEM((ACC_ROWS, B_HEADS * LANES), F32),
        ],
        compiler_params=_cparams("parallel", "arbitrary"),
        name="dsa",
    )(iqt, iw, bqt, ki, kb, vt)


MERGE_TM = 512


def _merge_kernel(h_ref, ya_ref, yb_ref, g_ref, wa_ref, wb_ref, wo_ref, o_ref):
    ta = jnp.dot(ya_ref[...], wa_ref[...], preferred_element_type=F32)
    tb = jnp.dot(yb_ref[...], wb_ref[...], preferred_element_type=F32)
    g = g_ref[...]
    mix = (g[:, :D_MODEL] * ta + g[:, D_MODEL:] * tb).astype(BF16)
    o_ref[...] = h_ref[...] + jnp.dot(mix, wo_ref[...], preferred_element_type=F32)


def _merge(h, ya, yb, g, wa, wb, wo):
    n = h.shape[0]
    row = lambda w: pl.BlockSpec((MERGE_TM, w), lambda i: (i, 0))
    full = lambda a: pl.BlockSpec(a.shape, lambda i: (0, 0))
    return pl.pallas_call(
        _merge_kernel,
        grid=(n // MERGE_TM,),
        in_specs=[row(D_MODEL), row(ya.shape[1]), row(yb.shape[1]), row(2 * D_MODEL),
                  full(wa), full(wb), full(wo)],
        out_specs=row(D_MODEL),
        out_shape=jax.ShapeDtypeStruct((n, D_MODEL), F32),
        compiler_params=_cparams("parallel"),
        name="merge",
    )(h, ya, yb, g, wa, wb, wo)


def _ple_kernel(h_ref, p_ref, lnp_ref, wg_ref, wp_ref, lnf_ref, o_ref):
    x = h_ref[...]
    u = _rms(x, lnp_ref[...]).astype(BF16)
    gate = jax.nn.sigmoid(jnp.dot(u, wg_ref[...], preferred_element_type=F32))
    proj = jnp.dot(p_ref[...].astype(BF16), wp_ref[...], preferred_element_type=F32)
    o_ref[...] = _rms(x + gate * proj, lnf_ref[...])


def _ple(h, p, lnp, wg, wp, lnf):
    n = h.shape[0]
    row = lambda w: pl.BlockSpec((MERGE_TM, w), lambda i: (i, 0))
    full = lambda a: pl.BlockSpec(a.shape, lambda i: (0, 0))
    return pl.pallas_call(
        _ple_kernel,
        grid=(n // MERGE_TM,),
        in_specs=[row(D_MODEL), row(D_PLE), full(lnp), full(wg), full(wp), full(lnf)],
        out_specs=row(D_MODEL),
        out_shape=jax.ShapeDtypeStruct((n, D_MODEL), F32),
        compiler_params=_cparams("parallel"),
        name="ple",
    )(h, p, lnp, wg, wp, lnf)


def kernel(x, p, ln_ffn1, w_ffn1_in, w_ffn1_out, ln_mix, w_in, a_sink, w_br_a, w_br_b, w_out,
           ln_ffn2, w_ffn2_in, w_ffn2_out, ln_ple, w_ple_gate, w_ple_proj, ln_final):
    bsz, t, d = x.shape
    assert p.shape[0] == 1, "the final norm is fused into the single layer's last kernel"
    n = bsz * t
    h = x.reshape(n, d)
    h = _ffn(h, ln_ffn1, w_ffn1_in[0].astype(BF16), w_ffn1_out[0].astype(BF16))
    aqt, ka, avt, iqt, bqt, ki, kb, vt, iw, g = _in_proj(h, ln_mix, _split_in_proj(w_in[0]))
    tok = lambda a: a.reshape(bsz, t, a.shape[-1])
    blk = lambda a: a.reshape(bsz, t // BLOCK, a.shape[-2], a.shape[-1])
    ya = _swa(a_sink[0], blk(aqt), tok(ka), blk(avt))
    yb = _dsa(blk(iqt), tok(iw), blk(bqt), tok(ki), tok(kb), blk(vt))
    h = _merge(h, ya.reshape(n, -1), yb.reshape(n, -1), g,
               w_br_a[0].astype(BF16), w_br_b[0].astype(BF16), w_out[0].astype(BF16))
    h = _ffn(h, ln_ffn2, w_ffn2_in[0].astype(BF16), w_ffn2_out[0].astype(BF16))
    h = _ple(h, p[0].reshape(n, -1), ln_ple, w_ple_gate[0].astype(BF16),
             w_ple_proj[0].astype(BF16), ln_final[None])
    return h.reshape(bsz, t, d)
```

```python
import functools

import numpy as np
import jax
import jax.numpy as jnp
from jax import lax
from jax.experimental import pallas as pl
from jax.experimental.pallas import tpu as pltpu

F32 = jnp.float32
BF16 = jnp.bfloat16
I32 = jnp.int32

D_MODEL = 1024
HEAD_DIM = 64
A_HEADS = 8
A_KV = 2
BLOCK = 128
B_HEADS = 8
IDX_HEADS = 8
TOPK = 256
D_FF = 2816
D_PLE = 256
EPS = 1e-6
NEG = -1e30

LANES = 128
VMEM_LIMIT = 48 * 1024 * 1024

_SLOPES = [float(np.float32(2.0) ** np.float32(-8.0 * i / (A_HEADS + B_HEADS)))
           for i in range(1, A_HEADS + B_HEADS + 1)]
SLOPES_A = _SLOPES[:A_HEADS]
SLOPES_B = _SLOPES[A_HEADS:]
LOG2E = float(np.log2(np.e))


def _bf16_terms(x, n=3):
    out = []
    for _ in range(n):
        t = float(np.float32(x).astype(jnp.bfloat16))
        out.append(t)
        x = x - t
    return out


SLOPE2_TERMS = [_bf16_terms(s * LOG2E) for s in SLOPES_B]


def _rms(x, g):
    return x * lax.rsqrt(jnp.mean(x * x, axis=-1, keepdims=True) + EPS) * g


def _cparams(*sem):
    return pltpu.CompilerParams(dimension_semantics=sem, vmem_limit_bytes=VMEM_LIMIT)


FFN_TM = 512


def _ffn_kernel(h_ref, ln_ref, wa_ref, wb_ref, wo_ref, o_ref):
    x = h_ref[...]
    xn = _rms(x, ln_ref[...]).astype(BF16)
    a = jnp.dot(xn, wa_ref[...], preferred_element_type=F32)
    b = jnp.dot(xn, wb_ref[...], preferred_element_type=F32)
    g = (a * jax.nn.sigmoid(a) * b).astype(BF16)
    o_ref[...] = x + 0.5 * jnp.dot(g, wo_ref[...], preferred_element_type=F32)


def _ffn(h, ln, w_in, w_out):
    n = h.shape[0]
    once = pl.Buffered(1)
    return pl.pallas_call(
        _ffn_kernel,
        grid=(n // FFN_TM,),
        in_specs=[
            pl.BlockSpec((FFN_TM, D_MODEL), lambda i: (i, 0)),
            pl.BlockSpec((1, D_MODEL), lambda i: (0, 0)),
            pl.BlockSpec((D_MODEL, D_FF), lambda i: (0, 0), pipeline_mode=once),
            pl.BlockSpec((D_MODEL, D_FF), lambda i: (0, 1), pipeline_mode=once),
            pl.BlockSpec((D_FF, D_MODEL), lambda i: (0, 0), pipeline_mode=once),
        ],
        out_specs=pl.BlockSpec((FFN_TM, D_MODEL), lambda i: (i, 0)),
        out_shape=jax.ShapeDtypeStruct((n, D_MODEL), F32),
        compiler_params=_cparams("parallel"),
        name="ffn",
    )(h, ln, w_in, w_in, w_out)


PROJ_TM = 256


def _inproj_kernel(blocks_per_seq, h_ref, ln_ref, waq, wka, wav, wiq, wbq, wki, wkb, wv, wiw, wg,
                   o_aqt, o_ka, o_avt, o_iqt, o_bqt, o_ki, o_kb, o_vt, o_iw, o_g):
    u = _rms(h_ref[...], ln_ref[...]).astype(BF16)

    def mm(w):
        return jnp.dot(u, w[...], preferred_element_type=F32)

    o_ka[...] = mm(wka).astype(BF16)
    o_iw[...] = mm(wiw)
    o_g[...] = jax.nn.sigmoid(mm(wg))

    row = lax.broadcasted_iota(I32, (PROJ_TM, LANES), 0)
    lane = lax.broadcasted_iota(I32, (PROJ_TM, LANES), 1)
    block = ((pl.program_id(0) * PROJ_TM + row) // BLOCK) % blocks_per_seq
    pos = jnp.where(lane < POS_LANE + 3, (row % BLOCK).astype(F32),
                    jnp.where(lane < POS_LANE + 6, block.astype(F32), 1.0))
    pos = jnp.where((lane >= POS_LANE) & (lane < POS_LANE + 9), pos, 0.0)
    o_ki[...] = mm(wki).astype(BF16)
    o_kb[...] = (mm(wkb) + pos).astype(BF16)
    v1 = jnp.where(lane == HEAD_DIM, 1.0, mm(wv))
    aq = mm(waq)
    av = mm(wav)
    iq = mm(wiq)
    bq = mm(wbq)
    row64 = lax.broadcasted_iota(I32, (HEAD_DIM, BLOCK), 0)
    zeros64 = jnp.zeros((HEAD_DIM, BLOCK), F32)
    for blk in range(PROJ_TM // BLOCK):
        tok = slice(blk * BLOCK, (blk + 1) * BLOCK)
        o_vt[blk] = v1[tok].T.astype(BF16)
        o_avt[blk] = av[tok].T.astype(BF16)
        for j in range(B_HEADS // 2):
            aq_t = aq[tok, j * LANES:(j + 1) * LANES].T
            iq_t = iq[tok, j * LANES:(j + 1) * LANES].T
            bq_t = bq[tok, j * LANES:(j + 1) * LANES].T
            for hh in range(2):
                h = 2 * j + hh
                cols = slice(h * BLOCK, (h + 1) * BLOCK)
                part = slice(hh * HEAD_DIM, (hh + 1) * HEAD_DIM)
                o_aqt[blk, :, cols] = jnp.concatenate([aq_t[part], zeros64], axis=0).astype(BF16)
                o_iqt[blk, :, cols] = jnp.concatenate([iq_t[part], zeros64], axis=0).astype(BF16)
                slope_rows = zeros64
                for r, term in enumerate(SLOPE2_TERMS[h]):
                    slope_rows = jnp.where(row64 == r, term, slope_rows)
                    slope_rows = jnp.where(row64 == r + 3, term * BLOCK, slope_rows)
                o_bqt[blk, :, cols] = jnp.concatenate([bq_t[part], slope_rows], axis=0).astype(BF16)


def _split_in_proj(w):
    scale = HEAD_DIM ** -0.5
    aq, ak, av = w[:, 0:512] * scale, w[:, 512:640], w[:, 640:768]
    bq, bk, bv = w[:, 768:1280] * (scale * LOG2E), w[:, 1280:1344], w[:, 1344:1408]
    iq, ik, iw = w[:, 1408:1920] * scale, w[:, 1920:1984], w[:, 1984:1992]
    g = w[:, 1992:4040]
    z64 = jnp.zeros((w.shape[0], HEAD_DIM), w.dtype)
    return dict(
        waq=aq.astype(BF16),
        wka=jnp.concatenate([ak[:, :HEAD_DIM], z64, ak[:, HEAD_DIM:], z64], axis=1).astype(BF16),
        wav=av.astype(BF16),
        wiq=iq.astype(BF16),
        wbq=bq.astype(BF16),
        wki=jnp.concatenate([ik, z64], axis=1).astype(BF16),
        wkb=jnp.concatenate([bk, z64], axis=1).astype(BF16),
        wv=jnp.concatenate([bv, z64], axis=1).astype(BF16),
        wiw=jnp.pad(iw, ((0, 0), (0, LANES - IDX_HEADS))).astype(BF16),
        wg=g.astype(BF16),
    )


def _in_proj(h, ln, ws, blocks_per_seq):
    n = h.shape[0]
    blocks = PROJ_TM // BLOCK
    names = ["waq", "wka", "wav", "wiq", "wbq", "wki", "wkb", "wv", "wiw", "wg"]
    rows = lambda wd, dt: (pl.BlockSpec((PROJ_TM, wd), lambda i: (i, 0)), jax.ShapeDtypeStruct((n, wd), dt))
    tiles = lambda r, c: (pl.BlockSpec((blocks, r, c), lambda i: (i, 0, 0)),
                          jax.ShapeDtypeStruct((n // BLOCK, r, c), BF16))
    outs = [tiles(LANES, A_HEADS * BLOCK), rows(A_KV * LANES, BF16), tiles(LANES, BLOCK),
            tiles(LANES, IDX_HEADS * BLOCK), tiles(LANES, B_HEADS * BLOCK),
            rows(LANES, BF16), rows(LANES, BF16), tiles(LANES, BLOCK),
            rows(LANES, F32), rows(2 * D_MODEL, F32)]
    return pl.pallas_call(
        functools.partial(_inproj_kernel, blocks_per_seq),
        grid=(n // PROJ_TM,),
        in_specs=[pl.BlockSpec((PROJ_TM, D_MODEL), lambda i: (i, 0)),
                  pl.BlockSpec((1, D_MODEL), lambda i: (0, 0))]
        + [pl.BlockSpec((D_MODEL, ws[k].shape[1]), lambda i: (0, 0)) for k in names],
        out_specs=[o[0] for o in outs],
        out_shape=[o[1] for o in outs],
        compiler_params=_cparams("parallel"),
        name="in_proj",
    )(h, ln, *[ws[k] for k in names])


def _swa_kernel(sink_ref, qt_ref, kp_ref, kc_ref, vtp_ref, vtc_ref, o_ref):
    n = pl.program_id(1)
    qt = qt_ref[0, 0]
    j = lax.broadcasted_iota(I32, (BLOCK, BLOCK), 0)
    i = lax.broadcasted_iota(I32, (BLOCK, BLOCK), 1)
    lower = j <= i
    dcur = (i - j).astype(F32)
    dprev = (i - j + BLOCK).astype(F32)
    prev_bias = jnp.where(n > 0, 0.0, NEG).astype(F32)
    per_group = A_HEADS // A_KV
    outs = []
    for g in range(A_KV):
        k2 = jnp.concatenate([kp_ref[0, :, g * LANES:(g + 1) * LANES],
                              kc_ref[0, :, g * LANES:(g + 1) * LANES]], axis=0)
        s2 = jnp.dot(k2, qt[:, g * per_group * BLOCK:(g + 1) * per_group * BLOCK],
                     preferred_element_type=F32)
        ps = []
        for r in range(per_group):
            h = g * per_group + r
            sp = s2[0:BLOCK, r * BLOCK:(r + 1) * BLOCK]
            sc = s2[BLOCK:2 * BLOCK, r * BLOCK:(r + 1) * BLOCK]
            s = jnp.where(lower, sc - SLOPES_A[h] * dcur, sp - SLOPES_A[h] * dprev + prev_bias)
            sink = sink_ref[h]
            m = jnp.maximum(jnp.max(s, axis=0, keepdims=True), sink)
            e = jnp.exp(s - m)
            p = e / (jnp.sum(e, axis=0, keepdims=True) + jnp.exp(sink - m))
            ps.append(jnp.concatenate([jnp.where(lower, 0.0, p), jnp.where(lower, p, 0.0)],
                                      axis=0).astype(BF16))
        rows = slice(g * HEAD_DIM, (g + 1) * HEAD_DIM)
        vt2 = jnp.concatenate([vtp_ref[0, 0, rows, :], vtc_ref[0, 0, rows, :]], axis=1)
        o_t = jnp.dot(vt2, jnp.concatenate(ps, axis=1), preferred_element_type=F32)
        for r in range(0, per_group, 2):
            pair = jnp.concatenate([o_t[:, r * BLOCK:(r + 1) * BLOCK],
                                    o_t[:, (r + 1) * BLOCK:(r + 2) * BLOCK]], axis=0)
            outs.append(pair.T)
    o_ref[0] = jnp.concatenate(outs, axis=1).astype(BF16)


def _swa(sink, aqt, ka, avt):
    bsz, nb = aqt.shape[:2]
    t = nb * BLOCK
    prev = lambda b, n: (b, jnp.maximum(n - 1, 0), 0)
    prev4 = lambda b, n: (b, jnp.maximum(n - 1, 0), 0, 0)
    return pl.pallas_call(
        _swa_kernel,
        grid=(bsz, nb),
        in_specs=[
            pl.BlockSpec(memory_space=pltpu.SMEM),
            pl.BlockSpec((1, 1, LANES, A_HEADS * BLOCK), lambda b, n: (b, n, 0, 0)),
            pl.BlockSpec((1, BLOCK, A_KV * LANES), prev),
            pl.BlockSpec((1, BLOCK, A_KV * LANES), lambda b, n: (b, n, 0)),
            pl.BlockSpec((1, 1, LANES, BLOCK), prev4),
            pl.BlockSpec((1, 1, LANES, BLOCK), lambda b, n: (b, n, 0, 0)),
        ],
        out_specs=pl.BlockSpec((1, BLOCK, A_HEADS * HEAD_DIM), lambda b, n: (b, n, 0)),
        out_shape=jax.ShapeDtypeStruct((bsz, t, A_HEADS * HEAD_DIM), BF16),
        compiler_params=_cparams("parallel", "arbitrary"),
        name="swa",
    )(sink, aqt, ka, ka, avt, avt)


CK = 256
SUB = 128
POS_LANE = HEAD_DIM
PIPE = 4
ACC_ROWS = 80
TINY = 1.1754944e-38
FREE_STEPS = 10
MIN_DENOMINATOR = 2.0 ** -100


def _tree_sum(xs):
    while len(xs) > 1:
        xs = [a + b for a, b in zip(xs[0::2], xs[1::2])]
    return xs[0]


def _for_chunks(nck, unroll, body, carry):
    def group(i, carry):
        for u in range(unroll):
            carry = body(i * unroll + u, carry)
        return carry
    nfull = nck // unroll
    carry = lax.fori_loop(0, nfull, group, carry)
    return lax.fori_loop(nfull * unroll, nck, body, carry)


def _dsa_kernel(iqt_ref, iw_ref, bqt_ref, ki_ref, kb_ref, vt_ref, kn_ref, yb_ref, sc_ref, thr_ref, s_ref, p_ref,
                acc_ref):
    n = pl.program_id(1)
    nck = n // (CK // BLOCK) + 1
    tq = n * BLOCK + lax.broadcasted_iota(I32, (CK, LANES), 1)
    rel = lax.broadcasted_iota(I32, (CK, LANES), 0)

    qi_t = iqt_ref[0, 0]
    iw_t = (iw_ref[0] * (IDX_HEADS ** -0.5)).T

    def score_chunk(c, carry):
        smin, smax = carry
        start = pl.multiple_of(c * CK, CK)
        s = jnp.dot(ki_ref[0, pl.ds(start, CK), :], qi_t, preferred_element_type=F32)
        acc = _tree_sum([jnp.maximum(s[:, h * LANES:(h + 1) * LANES], 0.0) * iw_t[h:h + 1, :]
                         for h in range(IDX_HEADS)])
        causal = c * CK + rel <= tq
        sc_ref[c] = jnp.where(causal, acc, NEG)
        smax = jnp.maximum(smax, jnp.max(jnp.where(causal, acc, NEG), axis=0, keepdims=True))
        smin = jnp.minimum(smin, jnp.min(jnp.where(causal, acc, -NEG), axis=0, keepdims=True))
        return smin, smax

    smin, smax = _for_chunks(nck, 4, score_chunk,
                             (jnp.full((1, LANES), -NEG, F32), jnp.full((1, LANES), NEG, F32)))

    thr_ref[...] = jnp.full(thr_ref.shape, NEG, F32)

    @pl.when(n >= 2)
    def _():
        def count_ge(probe):
            def body(c, cnt):
                hit = jnp.where(sc_ref[c] >= probe, 1.0, 0.0)
                return cnt + jnp.sum(hit.reshape(CK // 32, 4, 8, LANES), axis=0)
            cnt = _for_chunks(nck, 4, body, jnp.zeros((4, 8, LANES), F32))
            return jnp.sum(cnt.reshape(32, LANES), axis=0, keepdims=True)

        def probe_of(lo, hi):
            p = 0.5 * lo + 0.5 * hi
            p = jnp.where((lo < 0.0) & (hi > 0.0), 0.0, p)
            p = jnp.where((lo == 0.0) & (hi > TINY), TINY, p)
            return jnp.where((hi == 0.0) & (lo < -TINY), -TINY, p)

        def cond(st):
            lo, hi, c_lo, c_hi = st
            p = probe_of(lo, hi)
            done = (c_lo == TOPK) | (hi - lo <= TINY) | ~((p > lo) & (p < hi))
            return jnp.max(jnp.where(done, 0.0, 1.0)) > 0.0

        def step(st):
            lo, hi, c_lo, c_hi = st
            p = probe_of(lo, hi)
            c = count_ge(p)
            ge = c >= TOPK
            return (jnp.where(ge, p, lo), jnp.where(ge, hi, p),
                    jnp.where(ge, c, c_lo), jnp.where(ge, c_hi, c))

        n_causal = (tq[0:1, :] + 1).astype(F32)
        st0 = (smin, smax + (jnp.abs(smax) * 1e-6 + TINY), n_causal, jnp.zeros((1, LANES), F32))
        st1 = lax.fori_loop(0, FREE_STEPS, lambda _, st: step(st), st0)
        lo, hi, c_lo, c_hi = lax.while_loop(cond, lambda st: step(step(st)), st1)
        thr_ref[...] = jnp.broadcast_to(lo, thr_ref.shape)

        tied = c_lo > TOPK

        @pl.when(jnp.max(jnp.where(tied, 1.0, 0.0)) > 0.0)
        def _():
            keep = TOPK - c_hi
            tri = jnp.where(lax.broadcasted_iota(I32, (CK, CK), 0) >= lax.broadcasted_iota(I32, (CK, CK), 1),
                            1.0, 0.0).astype(BF16)

            def drop(c, before):
                x = sc_ref[c]
                eq = x == lo
                prefix = before + jnp.dot(tri, jnp.where(eq, 1.0, 0.0).astype(BF16),
                                          preferred_element_type=F32)
                sc_ref[c] = jnp.where(eq & tied & (prefix > keep), NEG, x)
                return prefix[CK - 1:CK, :]

            lax.fori_loop(0, nck, drop, jnp.zeros((1, LANES), F32))

    thr = thr_ref[0:1, :]
    rel_s = lax.broadcasted_iota(I32, (SUB, LANES), 0)
    tq_s = n * BLOCK + lax.broadcasted_iota(I32, (SUB, LANES), 1)
    last = pl.num_programs(1) - 1
    q2_t = bqt_ref[0, 0]

    def key_rows(t):
        return kb_ref[0, pl.ds(pl.multiple_of(jnp.minimum(t, last) * SUB, SUB), SUB), :]

    def value_rows(t):
        return vt_ref[0, jnp.clip(t, 0, last), 0:ACC_ROWS, :]

    def select_bias(t):
        tc = jnp.minimum(t, n)
        rows = pl.ds(pl.multiple_of((tc % 2) * SUB, SUB), SUB)
        sel = (sc_ref[tc // 2, rows, :] >= thr) & (t * SUB + rel_s <= tq_s)
        return jnp.where(sel, 0.0, NEG)

    qf = q2_t[0:HEAD_DIM].astype(F32)
    q_norm2 = jnp.sum(qf * qf, axis=0, keepdims=True)
    k_norm2 = jnp.concatenate([kn_ref[0, 0:1, :]] * B_HEADS, axis=1)
    tq_row = jnp.concatenate([tq_s[0:1].astype(F32)] * B_HEADS, axis=1)
    slope_row = jnp.concatenate([jnp.full((1, LANES), float(np.float32(sum(t))), F32) for t in SLOPE2_TERMS],
                                axis=1)
    offset = jnp.sqrt(q_norm2 * k_norm2) * (1.0 + 2.0 ** -6) + 2.0 ** -6 + slope_row * tq_row
    terms, rest = [], -offset
    for _ in range(3):
        terms.append(rest.astype(BF16).astype(F32))
        rest = rest - terms[-1]
    row_q = lax.broadcasted_iota(I32, (LANES, B_HEADS * LANES), 0)
    q2_off = q2_t.astype(F32)
    for r, term in enumerate(terms):
        q2_off = jnp.where(row_q == POS_LANE + 6 + r, term, q2_off)
    q2_off = q2_off.astype(BF16)

    def stage_logits(t, k):
        s_ref[k] = jnp.dot(key_rows(t), q2_off, preferred_element_type=F32)

    def probabilities(t, k):
        negb = select_bias(t)
        for h in range(B_HEADS):
            cols = slice(h * LANES, (h + 1) * LANES)
            p_ref[k, :, cols] = jnp.exp2(s_ref[k, :, cols] + negb).astype(BF16)

    def pv_accumulate(t, k):
        acc_ref[...] += jnp.dot(value_rows(t), p_ref[k], preferred_element_type=F32)

    def group(g, carry):
        for k in range(PIPE):
            pv_accumulate((g - 1) * PIPE + k, k)
            probabilities(g * PIPE + k, k)
            stage_logits((g + 1) * PIPE + k, k)
        return carry

    ngroups = n // PIPE + 1
    acc_ref[...] = jnp.zeros_like(acc_ref)
    for k in range(PIPE):
        stage_logits(k, k)
    p_ref[...] = jnp.zeros(p_ref.shape, BF16)
    lax.fori_loop(0, ngroups, group, 0)
    for k in range(PIPE):
        pv_accumulate((ngroups - 1) * PIPE + k, k)

    @pl.when(jnp.logical_not(jnp.min(acc_ref[HEAD_DIM:HEAD_DIM + 1, :]) > MIN_DENOMINATOR))
    def _():
        acc_ref[...] = jnp.zeros_like(acc_ref)

        def exact_step(t, m):
            s = jnp.dot(key_rows(t), q2_t, preferred_element_type=F32)
            negb = select_bias(t)
            ps, alphas, m_out = [], [], []
            for h in range(B_HEADS):
                l = s[:, h * LANES:(h + 1) * LANES] + negb
                m_new = jnp.maximum(m[h], jnp.max(l, axis=0, keepdims=True))
                alphas.append(jnp.exp2(m[h] - m_new))
                ps.append(jnp.exp2(l - m_new).astype(BF16))
                m_out.append(m_new)
            pv = jnp.dot(value_rows(t), jnp.concatenate(ps, axis=1), preferred_element_type=F32)
            acc_ref[...] = acc_ref[...] * jnp.concatenate(alphas, axis=1) + pv
            return m_out

        lax.fori_loop(0, n + 1, exact_step, [jnp.full((1, LANES), NEG, F32) for _ in range(B_HEADS)])

    outs = []
    pad = jnp.zeros((LANES - HEAD_DIM, LANES), F32)
    for h in range(B_HEADS):
        blk = acc_ref[:, h * LANES:(h + 1) * LANES]
        o_t = blk[0:HEAD_DIM] / blk[HEAD_DIM:HEAD_DIM + 1]
        outs.append(jnp.concatenate([o_t, pad], axis=0).T[:, 0:HEAD_DIM])
    yb_ref[0] = jnp.concatenate(outs, axis=1).astype(BF16)


def _knorm_kernel(kb_ref, o_ref):
    k = kb_ref[0].astype(F32)
    lane = lax.broadcasted_iota(I32, k.shape, 1)
    sq = jnp.where(lane < HEAD_DIM, k * k, 0.0)
    o_ref[0] = jnp.full(o_ref.shape[1:], jnp.max(jnp.sum(sq, axis=1, keepdims=True)), F32)


def _knorm(kb):
    bsz, t, _ = kb.shape
    return pl.pallas_call(
        _knorm_kernel,
        grid=(bsz,),
        in_specs=[pl.BlockSpec((1, t, LANES), lambda b: (b, 0, 0))],
        out_specs=pl.BlockSpec((1, 8, LANES), lambda b: (b, 0, 0)),
        out_shape=jax.ShapeDtypeStruct((bsz, 8, LANES), F32),
        compiler_params=_cparams("parallel"),
        name="knorm",
    )(kb)


def _dsa(iqt, iw, bqt, ki, kb, vt, kn):
    bsz, t, _ = iw.shape
    nb = t // BLOCK
    return pl.pallas_call(
        _dsa_kernel,
        grid=(bsz, nb),
        in_specs=[
            pl.BlockSpec((1, 1, LANES, IDX_HEADS * BLOCK), lambda b, n: (b, n, 0, 0)),
            pl.BlockSpec((1, BLOCK, LANES), lambda b, n: (b, n, 0)),
            pl.BlockSpec((1, 1, LANES, B_HEADS * BLOCK), lambda b, n: (b, n, 0, 0)),
            pl.BlockSpec((1, t, LANES), lambda b, n: (b, 0, 0)),
            pl.BlockSpec((1, t, LANES), lambda b, n: (b, 0, 0)),
            pl.BlockSpec((1, nb, LANES, BLOCK), lambda b, n: (b, 0, 0, 0)),
            pl.BlockSpec((1, 8, LANES), lambda b, n: (b, 0, 0)),
        ],
        out_specs=pl.BlockSpec((1, BLOCK, B_HEADS * HEAD_DIM), lambda b, n: (b, n, 0)),
        out_shape=jax.ShapeDtypeStruct((bsz, t, B_HEADS * HEAD_DIM), BF16),
        scratch_shapes=[
            pltpu.VMEM((t // CK, CK, LANES), F32),
            pltpu.VMEM((8, LANES), F32),
            pltpu.VMEM((PIPE, SUB, B_HEADS * LANES), F32),
            pltpu.VMEM((PIPE, SUB, B_HEADS * LANES), BF16),
            pltpu.VMEM((ACC_ROWS, B_HEADS * LANES), F32),
        ],
        compiler_params=_cparams("parallel", "arbitrary"),
        name="dsa",
    )(iqt, iw, bqt, ki, kb, vt, kn)


MERGE_TM = 512


def _merge_kernel(h_ref, ya_ref, yb_ref, g_ref, wa_ref, wb_ref, wo_ref, o_ref):
    ta = jnp.dot(ya_ref[...], wa_ref[...], preferred_element_type=F32)
    tb = jnp.dot(yb_ref[...], wb_ref[...], preferred_element_type=F32)
    g = g_ref[...]
    mix = (g[:, :D_MODEL] * ta + g[:, D_MODEL:] * tb).astype(BF16)
    o_ref[...] = h_ref[...] + jnp.dot(mix, wo_ref[...], preferred_element_type=F32)


def _merge(h, ya, yb, g, wa, wb, wo):
    n = h.shape[0]
    row = lambda w: pl.BlockSpec((MERGE_TM, w), lambda i: (i, 0))
    full = lambda a: pl.BlockSpec(a.shape, lambda i: (0, 0))
    return pl.pallas_call(
        _merge_kernel,
        grid=(n // MERGE_TM,),
        in_specs=[row(D_MODEL), row(ya.shape[1]), row(yb.shape[1]), row(2 * D_MODEL),
                  full(wa), full(wb), full(wo)],
        out_specs=row(D_MODEL),
        out_shape=jax.ShapeDtypeStruct((n, D_MODEL), F32),
        compiler_params=_cparams("parallel"),
        name="merge",
    )(h, ya, yb, g, wa, wb, wo)


def _ple_kernel(h_ref, p_ref, lnp_ref, wg_ref, wp_ref, lnf_ref, o_ref):
    x = h_ref[...]
    u = _rms(x, lnp_ref[...]).astype(BF16)
    gate = jax.nn.sigmoid(jnp.dot(u, wg_ref[...], preferred_element_type=F32))
    proj = jnp.dot(p_ref[...].astype(BF16), wp_ref[...], preferred_element_type=F32)
    o_ref[...] = _rms(x + gate * proj, lnf_ref[...])


def _ple(h, p, lnp, wg, wp, lnf):
    n = h.shape[0]
    row = lambda w: pl.BlockSpec((MERGE_TM, w), lambda i: (i, 0))
    full = lambda a: pl.BlockSpec(a.shape, lambda i: (0, 0))
    return pl.pallas_call(
        _ple_kernel,
        grid=(n // MERGE_TM,),
        in_specs=[row(D_MODEL), row(D_PLE), full(lnp), full(wg), full(wp), full(lnf)],
        out_specs=row(D_MODEL),
        out_shape=jax.ShapeDtypeStruct((n, D_MODEL), F32),
        compiler_params=_cparams("parallel"),
        name="ple",
    )(h, p, lnp, wg, wp, lnf)


def kernel(x, p, ln_ffn1, w_ffn1_in, w_ffn1_out, ln_mix, w_in, a_sink, w_br_a, w_br_b, w_out,
           ln_ffn2, w_ffn2_in, w_ffn2_out, ln_ple, w_ple_gate, w_ple_proj, ln_final):
    bsz, t, d = x.shape
    assert p.shape[0] == 1, "the final norm is fused into the single layer's last kernel"
    n = bsz * t
    h = x.reshape(n, d)
    h = _ffn(h, ln_ffn1, w_ffn1_in[0].astype(BF16), w_ffn1_out[0].astype(BF16))
    aqt, ka, avt, iqt, bqt, ki, kb, vt, iw, g = _in_proj(h, ln_mix, _split_in_proj(w_in[0]), t // BLOCK)
    tok = lambda a: a.reshape(bsz, t, a.shape[-1])
    blk = lambda a: a.reshape(bsz, t // BLOCK, a.shape[-2], a.shape[-1])
    ya = _swa(a_sink[0], blk(aqt), tok(ka), blk(avt))
    yb = _dsa(blk(iqt), tok(iw), blk(bqt), tok(ki), tok(kb), blk(vt), _knorm(tok(kb)))
    h = _merge(h, ya.reshape(n, -1), yb.reshape(n, -1), g,
               w_br_a[0].astype(BF16), w_br_b[0].astype(BF16), w_out[0].astype(BF16))
    h = _ffn(h, ln_ffn2, w_ffn2_in[0].astype(BF16), w_ffn2_out[0].astype(BF16))
    h = _ple(h, p[0].reshape(n, -1), ln_ple, w_ple_gate[0].astype(BF16),
             w_ple_proj[0].astype(BF16), ln_final[None])
    return h.reshape(bsz, t, d)
```

```python
import functools

import numpy as np
import jax
import jax.numpy as jnp
from jax import lax
from jax.experimental import pallas as pl
from jax.experimental.pallas import tpu as pltpu

F32 = jnp.float32
BF16 = jnp.bfloat16
I32 = jnp.int32

D_MODEL = 1024
HEAD_DIM = 64
A_HEADS = 8
A_KV = 2
BLOCK = 128
B_HEADS = 8
IDX_HEADS = 8
TOPK = 256
D_FF = 2816
D_PLE = 256
EPS = 1e-6
NEG = -1e30

LANES = 128
VMEM_LIMIT = 48 * 1024 * 1024

_SLOPES = [float(np.float32(2.0) ** np.float32(-8.0 * i / (A_HEADS + B_HEADS)))
           for i in range(1, A_HEADS + B_HEADS + 1)]
SLOPES_A = _SLOPES[:A_HEADS]
SLOPES_B = _SLOPES[A_HEADS:]
LOG2E = float(np.log2(np.e))


def _bf16_terms(x, n=3):
    out = []
    for _ in range(n):
        t = float(np.float32(x).astype(jnp.bfloat16))
        out.append(t)
        x = x - t
    return out


SLOPE2_TERMS = [_bf16_terms(s * LOG2E) for s in SLOPES_B]


def _rms(x, g):
    return x * lax.rsqrt(jnp.mean(x * x, axis=-1, keepdims=True) + EPS) * g


def _cparams(*sem):
    return pltpu.CompilerParams(dimension_semantics=sem, vmem_limit_bytes=VMEM_LIMIT)


FFN_TM = 512


def _ffn_kernel(h_ref, ln_ref, wa_ref, wb_ref, wo_ref, o_ref):
    x = h_ref[...]
    xn = _rms(x, ln_ref[...]).astype(BF16)
    a = jnp.dot(xn, wa_ref[...], preferred_element_type=F32)
    b = jnp.dot(xn, wb_ref[...], preferred_element_type=F32)
    g = (a * jax.nn.sigmoid(a) * b).astype(BF16)
    o_ref[...] = x + 0.5 * jnp.dot(g, wo_ref[...], preferred_element_type=F32)


def _ffn(h, ln, w_in, w_out):
    n = h.shape[0]
    once = pl.Buffered(1)
    return pl.pallas_call(
        _ffn_kernel,
        grid=(n // FFN_TM,),
        in_specs=[
            pl.BlockSpec((FFN_TM, D_MODEL), lambda i: (i, 0)),
            pl.BlockSpec((1, D_MODEL), lambda i: (0, 0)),
            pl.BlockSpec((D_MODEL, D_FF), lambda i: (0, 0), pipeline_mode=once),
            pl.BlockSpec((D_MODEL, D_FF), lambda i: (0, 1), pipeline_mode=once),
            pl.BlockSpec((D_FF, D_MODEL), lambda i: (0, 0), pipeline_mode=once),
        ],
        out_specs=pl.BlockSpec((FFN_TM, D_MODEL), lambda i: (i, 0)),
        out_shape=jax.ShapeDtypeStruct((n, D_MODEL), F32),
        compiler_params=_cparams("parallel"),
        name="ffn",
    )(h, ln, w_in, w_in, w_out)


PROJ_TM = 256


def _inproj_kernel(blocks_per_seq, h_ref, ln_ref, waq, wka, wav, wiq, wbq, wki, wkb, wv, wiw, wg,
                   o_aqt, o_ka, o_avt, o_iqt, o_bqt, o_ki, o_kb, o_vt, o_iw, o_g):
    u = _rms(h_ref[...], ln_ref[...]).astype(BF16)

    def mm(w):
        return jnp.dot(u, w[...], preferred_element_type=F32)

    o_ka[...] = mm(wka).astype(BF16)
    o_iw[...] = mm(wiw)
    o_g[...] = jax.nn.sigmoid(mm(wg))

    row = lax.broadcasted_iota(I32, (PROJ_TM, LANES), 0)
    lane = lax.broadcasted_iota(I32, (PROJ_TM, LANES), 1)
    block = ((pl.program_id(0) * PROJ_TM + row) // BLOCK) % blocks_per_seq
    pos = jnp.where(lane < POS_LANE + 3, (row % BLOCK).astype(F32),
                    jnp.where(lane < POS_LANE + 6, block.astype(F32), 1.0))
    pos = jnp.where((lane >= POS_LANE) & (lane < POS_LANE + 9), pos, 0.0)
    o_ki[...] = mm(wki).astype(BF16)
    o_kb[...] = (mm(wkb) + pos).astype(BF16)
    v1 = jnp.where(lane == HEAD_DIM, 1.0, mm(wv))
    aq = mm(waq)
    av = mm(wav)
    iq = mm(wiq)
    bq = mm(wbq)
    row64 = lax.broadcasted_iota(I32, (HEAD_DIM, BLOCK), 0)
    zeros64 = jnp.zeros((HEAD_DIM, BLOCK), F32)
    for blk in range(PROJ_TM // BLOCK):
        tok = slice(blk * BLOCK, (blk + 1) * BLOCK)
        o_vt[blk] = v1[tok].T.astype(BF16)
        o_avt[blk] = av[tok].T.astype(BF16)
        for j in range(B_HEADS // 2):
            aq_t = aq[tok, j * LANES:(j + 1) * LANES].T
            iq_t = iq[tok, j * LANES:(j + 1) * LANES].T
            bq_t = bq[tok, j * LANES:(j + 1) * LANES].T
            for hh in range(2):
                h = 2 * j + hh
                cols = slice(h * BLOCK, (h + 1) * BLOCK)
                part = slice(hh * HEAD_DIM, (hh + 1) * HEAD_DIM)
                o_aqt[blk, :, cols] = jnp.concatenate([aq_t[part], zeros64], axis=0).astype(BF16)
                o_iqt[blk, :, cols] = jnp.concatenate([iq_t[part], zeros64], axis=0).astype(BF16)
                slope_rows = zeros64
                for r, term in enumerate(SLOPE2_TERMS[h]):
                    slope_rows = jnp.where(row64 == r, term, slope_rows)
                    slope_rows = jnp.where(row64 == r + 3, term * BLOCK, slope_rows)
                o_bqt[blk, :, cols] = jnp.concatenate([bq_t[part], slope_rows], axis=0).astype(BF16)


def _split_in_proj(w):
    scale = HEAD_DIM ** -0.5
    aq, ak, av = w[:, 0:512] * scale, w[:, 512:640], w[:, 640:768]
    bq, bk, bv = w[:, 768:1280] * (scale * LOG2E), w[:, 1280:1344], w[:, 1344:1408]
    iq, ik, iw = w[:, 1408:1920] * scale, w[:, 1920:1984], w[:, 1984:1992]
    g = w[:, 1992:4040]
    z64 = jnp.zeros((w.shape[0], HEAD_DIM), w.dtype)
    return dict(
        waq=aq.astype(BF16),
        wka=jnp.concatenate([ak[:, :HEAD_DIM], z64, ak[:, HEAD_DIM:], z64], axis=1).astype(BF16),
        wav=av.astype(BF16),
        wiq=iq.astype(BF16),
        wbq=bq.astype(BF16),
        wki=jnp.concatenate([ik, z64], axis=1).astype(BF16),
        wkb=jnp.concatenate([bk, z64], axis=1).astype(BF16),
        wv=jnp.concatenate([bv, z64], axis=1).astype(BF16),
        wiw=jnp.pad(iw, ((0, 0), (0, LANES - IDX_HEADS))).astype(BF16),
        wg=g.astype(BF16),
    )


def _in_proj(h, ln, ws, blocks_per_seq):
    n = h.shape[0]
    blocks = PROJ_TM // BLOCK
    names = ["waq", "wka", "wav", "wiq", "wbq", "wki", "wkb", "wv", "wiw", "wg"]
    rows = lambda wd, dt: (pl.BlockSpec((PROJ_TM, wd), lambda i: (i, 0)), jax.ShapeDtypeStruct((n, wd), dt))
    tiles = lambda r, c: (pl.BlockSpec((blocks, r, c), lambda i: (i, 0, 0)),
                          jax.ShapeDtypeStruct((n // BLOCK, r, c), BF16))
    outs = [tiles(LANES, A_HEADS * BLOCK), rows(A_KV * LANES, BF16), tiles(LANES, BLOCK),
            tiles(LANES, IDX_HEADS * BLOCK), tiles(LANES, B_HEADS * BLOCK),
            rows(LANES, BF16), rows(LANES, BF16), tiles(LANES, BLOCK),
            rows(LANES, F32), rows(2 * D_MODEL, F32)]
    return pl.pallas_call(
        functools.partial(_inproj_kernel, blocks_per_seq),
        grid=(n // PROJ_TM,),
        in_specs=[pl.BlockSpec((PROJ_TM, D_MODEL), lambda i: (i, 0)),
                  pl.BlockSpec((1, D_MODEL), lambda i: (0, 0))]
        + [pl.BlockSpec((D_MODEL, ws[k].shape[1]), lambda i: (0, 0)) for k in names],
        out_specs=[o[0] for o in outs],
        out_shape=[o[1] for o in outs],
        compiler_params=_cparams("parallel"),
        name="in_proj",
    )(h, ln, *[ws[k] for k in names])


def _swa_kernel(sink_ref, qt_ref, kp_ref, kc_ref, vtp_ref, vtc_ref, o_ref):
    n = pl.program_id(1)
    qt = qt_ref[0, 0]
    j = lax.broadcasted_iota(I32, (BLOCK, BLOCK), 0)
    i = lax.broadcasted_iota(I32, (BLOCK, BLOCK), 1)
    lower = j <= i
    dcur = (i - j).astype(F32)
    dprev = (i - j + BLOCK).astype(F32)
    prev_bias = jnp.where(n > 0, 0.0, NEG).astype(F32)
    per_group = A_HEADS // A_KV
    outs = []
    for g in range(A_KV):
        k2 = jnp.concatenate([kp_ref[0, :, g * LANES:(g + 1) * LANES],
                              kc_ref[0, :, g * LANES:(g + 1) * LANES]], axis=0)
        s2 = jnp.dot(k2, qt[:, g * per_group * BLOCK:(g + 1) * per_group * BLOCK],
                     preferred_element_type=F32)
        ps = []
        for r in range(per_group):
            h = g * per_group + r
            sp = s2[0:BLOCK, r * BLOCK:(r + 1) * BLOCK]
            sc = s2[BLOCK:2 * BLOCK, r * BLOCK:(r + 1) * BLOCK]
            s = jnp.where(lower, sc - SLOPES_A[h] * dcur, sp - SLOPES_A[h] * dprev + prev_bias)
            sink = sink_ref[h]
            m = jnp.maximum(jnp.max(s, axis=0, keepdims=True), sink)
            e = jnp.exp(s - m)
            p = e / (jnp.sum(e, axis=0, keepdims=True) + jnp.exp(sink - m))
            ps.append(jnp.concatenate([jnp.where(lower, 0.0, p), jnp.where(lower, p, 0.0)],
                                      axis=0).astype(BF16))
        rows = slice(g * HEAD_DIM, (g + 1) * HEAD_DIM)
        vt2 = jnp.concatenate([vtp_ref[0, 0, rows, :], vtc_ref[0, 0, rows, :]], axis=1)
        o_t = jnp.dot(vt2, jnp.concatenate(ps, axis=1), preferred_element_type=F32)
        for r in range(0, per_group, 2):
            pair = jnp.concatenate([o_t[:, r * BLOCK:(r + 1) * BLOCK],
                                    o_t[:, (r + 1) * BLOCK:(r + 2) * BLOCK]], axis=0)
            outs.append(pair.T)
    o_ref[0] = jnp.concatenate(outs, axis=1).astype(BF16)


def _swa(sink, aqt, ka, avt):
    bsz, nb = aqt.shape[:2]
    t = nb * BLOCK
    prev = lambda b, n: (b, jnp.maximum(n - 1, 0), 0)
    prev4 = lambda b, n: (b, jnp.maximum(n - 1, 0), 0, 0)
    return pl.pallas_call(
        _swa_kernel,
        grid=(bsz, nb),
        in_specs=[
            pl.BlockSpec(memory_space=pltpu.SMEM),
            pl.BlockSpec((1, 1, LANES, A_HEADS * BLOCK), lambda b, n: (b, n, 0, 0)),
            pl.BlockSpec((1, BLOCK, A_KV * LANES), prev),
            pl.BlockSpec((1, BLOCK, A_KV * LANES), lambda b, n: (b, n, 0)),
            pl.BlockSpec((1, 1, LANES, BLOCK), prev4),
            pl.BlockSpec((1, 1, LANES, BLOCK), lambda b, n: (b, n, 0, 0)),
        ],
        out_specs=pl.BlockSpec((1, BLOCK, A_HEADS * HEAD_DIM), lambda b, n: (b, n, 0)),
        out_shape=jax.ShapeDtypeStruct((bsz, t, A_HEADS * HEAD_DIM), BF16),
        compiler_params=_cparams("parallel", "arbitrary"),
        name="swa",
    )(sink, aqt, ka, ka, avt, avt)


CK = 256
SUB = 128
POS_LANE = HEAD_DIM
PIPE = 4
ACC_ROWS = 80
TINY = 1.1754944e-38
FREE_STEPS = 14
MIN_DENOMINATOR = 2.0 ** -100


def _tree_sum(xs):
    while len(xs) > 1:
        xs = [a + b for a, b in zip(xs[0::2], xs[1::2])]
    return xs[0]


def _for_chunks(nck, unroll, body, carry):
    def group(i, carry):
        for u in range(unroll):
            carry = body(i * unroll + u, carry)
        return carry
    nfull = nck // unroll
    carry = lax.fori_loop(0, nfull, group, carry)
    return lax.fori_loop(nfull * unroll, nck, body, carry)


def _dsa_kernel(iqt_ref, iw_ref, bqt_ref, ki_ref, kb_ref, vt_ref, kn_ref, yb_ref, sc_ref, thr_ref, s_ref, p_ref,
                acc_ref):
    n = pl.program_id(1)
    nck = n // (CK // BLOCK) + 1
    tq = n * BLOCK + lax.broadcasted_iota(I32, (CK, LANES), 1)
    rel = lax.broadcasted_iota(I32, (CK, LANES), 0)

    qi_t = iqt_ref[0, 0]
    iw_t = (iw_ref[0] * (IDX_HEADS ** -0.5)).T

    def score_chunk(c, carry):
        smin, smax = carry
        start = pl.multiple_of(c * CK, CK)
        s = jnp.dot(ki_ref[0, pl.ds(start, CK), :], qi_t, preferred_element_type=F32)
        acc = _tree_sum([jnp.maximum(s[:, h * LANES:(h + 1) * LANES], 0.0) * iw_t[h:h + 1, :]
                         for h in range(IDX_HEADS)])
        causal = c * CK + rel <= tq
        sc_ref[c] = jnp.where(causal, acc, NEG)
        smax = jnp.maximum(smax, jnp.max(jnp.where(causal, acc, NEG), axis=0, keepdims=True))
        smin = jnp.minimum(smin, jnp.min(jnp.where(causal, acc, -NEG), axis=0, keepdims=True))
        return smin, smax

    smin, smax = _for_chunks(nck, 4, score_chunk,
                             (jnp.full((1, LANES), -NEG, F32), jnp.full((1, LANES), NEG, F32)))

    thr_ref[...] = jnp.full(thr_ref.shape, NEG, F32)

    @pl.when(n >= 2)
    def _():
        def count_ge(probe):
            def body(c, cnt):
                hit = jnp.where(sc_ref[c] >= probe, 1.0, 0.0)
                return cnt + jnp.sum(hit.reshape(CK // 32, 4, 8, LANES), axis=0)
            cnt = _for_chunks(nck, 4, body, jnp.zeros((4, 8, LANES), F32))
            return jnp.sum(cnt.reshape(32, LANES), axis=0, keepdims=True)

        def probe_of(lo, hi):
            p = 0.5 * lo + 0.5 * hi
            p = jnp.where((lo < 0.0) & (hi > 0.0), 0.0, p)
            p = jnp.where((lo == 0.0) & (hi > TINY), TINY, p)
            return jnp.where((hi == 0.0) & (lo < -TINY), -TINY, p)

        def cond(st):
            lo, hi, c_lo, c_hi = st
            p = probe_of(lo, hi)
            done = (c_lo == TOPK) | (hi - lo <= TINY) | ~((p > lo) & (p < hi))
            return jnp.max(jnp.where(done, 0.0, 1.0)) > 0.0

        def step(st):
            lo, hi, c_lo, c_hi = st
            p = probe_of(lo, hi)
            c = count_ge(p)
            ge = c >= TOPK
            return (jnp.where(ge, p, lo), jnp.where(ge, hi, p),
                    jnp.where(ge, c, c_lo), jnp.where(ge, c_hi, c))

        n_causal = (tq[0:1, :] + 1).astype(F32)
        st0 = (smin, smax + (jnp.abs(smax) * 1e-6 + TINY), n_causal, jnp.zeros((1, LANES), F32))
        st1 = lax.fori_loop(0, FREE_STEPS, lambda _, st: step(st), st0)
        lo, hi, c_lo, c_hi = lax.while_loop(cond, lambda st: step(step(st)), st1)
        thr_ref[...] = jnp.broadcast_to(lo, thr_ref.shape)

        tied = c_lo > TOPK

        @pl.when(jnp.max(jnp.where(tied, 1.0, 0.0)) > 0.0)
        def _():
            keep = TOPK - c_hi
            tri = jnp.where(lax.broadcasted_iota(I32, (CK, CK), 0) >= lax.broadcasted_iota(I32, (CK, CK), 1),
                            1.0, 0.0).astype(BF16)

            def drop(c, before):
                x = sc_ref[c]
                eq = x == lo
                prefix = before + jnp.dot(tri, jnp.where(eq, 1.0, 0.0).astype(BF16),
                                          preferred_element_type=F32)
                sc_ref[c] = jnp.where(eq & tied & (prefix > keep), NEG, x)
                return prefix[CK - 1:CK, :]

            lax.fori_loop(0, nck, drop, jnp.zeros((1, LANES), F32))

    thr = thr_ref[0:1, :]
    rel_s = lax.broadcasted_iota(I32, (SUB, LANES), 0)
    tq_s = n * BLOCK + lax.broadcasted_iota(I32, (SUB, LANES), 1)
    last = pl.num_programs(1) - 1
    q2_t = bqt_ref[0, 0]

    def key_rows(t):
        return kb_ref[0, pl.ds(pl.multiple_of(jnp.minimum(t, last) * SUB, SUB), SUB), :]

    def value_rows(t):
        return vt_ref[0, jnp.clip(t, 0, last), 0:ACC_ROWS, :]

    def select_bias(t):
        tc = jnp.minimum(t, n)
        rows = pl.ds(pl.multiple_of((tc % 2) * SUB, SUB), SUB)
        sel = (sc_ref[tc // 2, rows, :] >= thr) & (t * SUB + rel_s <= tq_s)
        return jnp.where(sel, 0.0, NEG)

    qf = q2_t[0:HEAD_DIM].astype(F32)
    q_norm2 = jnp.sum(qf * qf, axis=0, keepdims=True)
    k_norm2 = jnp.concatenate([kn_ref[0, 0:1, :]] * B_HEADS, axis=1)
    tq_row = jnp.concatenate([tq_s[0:1].astype(F32)] * B_HEADS, axis=1)
    slope_row = jnp.concatenate([jnp.full((1, LANES), float(np.float32(sum(t))), F32) for t in SLOPE2_TERMS],
                                axis=1)
    offset = jnp.sqrt(q_norm2 * k_norm2) * (1.0 + 2.0 ** -6) + 2.0 ** -6 + slope_row * tq_row
    terms, rest = [], -offset
    for _ in range(3):
        terms.append(rest.astype(BF16).astype(F32))
        rest = rest - terms[-1]
    row_q = lax.broadcasted_iota(I32, (LANES, B_HEADS * LANES), 0)
    q2_off = q2_t.astype(F32)
    for r, term in enumerate(terms):
        q2_off = jnp.where(row_q == POS_LANE + 6 + r, term, q2_off)
    q2_off = q2_off.astype(BF16)

    def stage_logits(t, k):
        s_ref[k] = jnp.dot(key_rows(t), q2_off, preferred_element_type=F32)

    def probabilities(t, k):
        negb = select_bias(t)
        for h in range(B_HEADS):
            cols = slice(h * LANES, (h + 1) * LANES)
            p_ref[k, :, cols] = jnp.exp2(s_ref[k, :, cols] + negb).astype(BF16)

    def pv_accumulate(t, k):
        values = jnp.concatenate([value_rows(t), value_rows(t + 1)], axis=1)
        probs = jnp.concatenate([p_ref[k], p_ref[k + 1]], axis=0)
        acc_ref[...] += jnp.dot(values, probs, preferred_element_type=F32)

    def group(g, carry):
        for k in range(0, PIPE, 2):
            pv_accumulate((g - 1) * PIPE + k, k)
            for kk in (k, k + 1):
                probabilities(g * PIPE + kk, kk)
                stage_logits((g + 1) * PIPE + kk, kk)
        return carry

    ngroups = n // PIPE + 1
    acc_ref[...] = jnp.zeros_like(acc_ref)
    for k in range(PIPE):
        stage_logits(k, k)
    p_ref[...] = jnp.zeros(p_ref.shape, BF16)
    lax.fori_loop(0, ngroups, group, 0)
    for k in range(0, PIPE, 2):
        pv_accumulate((ngroups - 1) * PIPE + k, k)

    @pl.when(jnp.logical_not(jnp.min(acc_ref[HEAD_DIM:HEAD_DIM + 1, :]) > MIN_DENOMINATOR))
    def _():
        acc_ref[...] = jnp.zeros_like(acc_ref)

        def exact_step(t, m):
            s = jnp.dot(key_rows(t), q2_t, preferred_element_type=F32)
            negb = select_bias(t)
            ps, alphas, m_out = [], [], []
            for h in range(B_HEADS):
                l = s[:, h * LANES:(h + 1) * LANES] + negb
                m_new = jnp.maximum(m[h], jnp.max(l, axis=0, keepdims=True))
                alphas.append(jnp.exp2(m[h] - m_new))
                ps.append(jnp.exp2(l - m_new).astype(BF16))
                m_out.append(m_new)
            pv = jnp.dot(value_rows(t), jnp.concatenate(ps, axis=1), preferred_element_type=F32)
            acc_ref[...] = acc_ref[...] * jnp.concatenate(alphas, axis=1) + pv
            return m_out

        lax.fori_loop(0, n + 1, exact_step, [jnp.full((1, LANES), NEG, F32) for _ in range(B_HEADS)])

    outs = []
    pad = jnp.zeros((LANES - HEAD_DIM, LANES), F32)
    for h in range(B_HEADS):
        blk = acc_ref[:, h * LANES:(h + 1) * LANES]
        o_t = blk[0:HEAD_DIM] / blk[HEAD_DIM:HEAD_DIM + 1]
        outs.append(jnp.concatenate([o_t, pad], axis=0).T[:, 0:HEAD_DIM])
    yb_ref[0] = jnp.concatenate(outs, axis=1).astype(BF16)


def _knorm_kernel(kb_ref, o_ref):
    k = kb_ref[0].astype(F32)
    lane = lax.broadcasted_iota(I32, k.shape, 1)
    sq = jnp.where(lane < HEAD_DIM, k * k, 0.0)
    o_ref[0] = jnp.full(o_ref.shape[1:], jnp.max(jnp.sum(sq, axis=1, keepdims=True)), F32)


def _knorm(kb):
    bsz, t, _ = kb.shape
    return pl.pallas_call(
        _knorm_kernel,
        grid=(bsz,),
        in_specs=[pl.BlockSpec((1, t, LANES), lambda b: (b, 0, 0))],
        out_specs=pl.BlockSpec((1, 8, LANES), lambda b: (b, 0, 0)),
        out_shape=jax.ShapeDtypeStruct((bsz, 8, LANES), F32),
        compiler_params=_cparams("parallel"),
        name="knorm",
    )(kb)


def _dsa(iqt, iw, bqt, ki, kb, vt, kn):
    bsz, t, _ = iw.shape
    nb = t // BLOCK
    return pl.pallas_call(
        _dsa_kernel,
        grid=(bsz, nb),
        in_specs=[
            pl.BlockSpec((1, 1, LANES, IDX_HEADS * BLOCK), lambda b, n: (b, n, 0, 0)),
            pl.BlockSpec((1, BLOCK, LANES), lambda b, n: (b, n, 0)),
            pl.BlockSpec((1, 1, LANES, B_HEADS * BLOCK), lambda b, n: (b, n, 0, 0)),
            pl.BlockSpec((1, t, LANES), lambda b, n: (b, 0, 0)),
            pl.BlockSpec((1, t, LANES), lambda b, n: (b, 0, 0)),
            pl.BlockSpec((1, nb, LANES, BLOCK), lambda b, n: (b, 0, 0, 0)),
            pl.BlockSpec((1, 8, LANES), lambda b, n: (b, 0, 0)),
        ],
        out_specs=pl.BlockSpec((1, BLOCK, B_HEADS * HEAD_DIM), lambda b, n: (b, n, 0)),
        out_shape=jax.ShapeDtypeStruct((bsz, t, B_HEADS * HEAD_DIM), BF16),
        scratch_shapes=[
            pltpu.VMEM((t // CK, CK, LANES), F32),
            pltpu.VMEM((8, LANES), F32),
            pltpu.VMEM((PIPE, SUB, B_HEADS * LANES), F32),
            pltpu.VMEM((PIPE, SUB, B_HEADS * LANES), BF16),
            pltpu.VMEM((ACC_ROWS, B_HEADS * LANES), F32),
        ],
        compiler_params=_cparams("parallel", "arbitrary"),
        name="dsa",
    )(iqt, iw, bqt, ki, kb, vt, kn)


MERGE_TM = 512


def _merge_kernel(h_ref, ya_ref, yb_ref, g_ref, wa_ref, wb_ref, wo_ref, o_ref):
    ta = jnp.dot(ya_ref[...], wa_ref[...], preferred_element_type=F32)
    tb = jnp.dot(yb_ref[...], wb_ref[...], preferred_element_type=F32)
    g = g_ref[...]
    mix = (g[:, :D_MODEL] * ta + g[:, D_MODEL:] * tb).astype(BF16)
    o_ref[...] = h_ref[...] + jnp.dot(mix, wo_ref[...], preferred_element_type=F32)


def _merge(h, ya, yb, g, wa, wb, wo):
    n = h.shape[0]
    row = lambda w: pl.BlockSpec((MERGE_TM, w), lambda i: (i, 0))
    full = lambda a: pl.BlockSpec(a.shape, lambda i: (0, 0))
    return pl.pallas_call(
        _merge_kernel,
        grid=(n // MERGE_TM,),
        in_specs=[row(D_MODEL), row(ya.shape[1]), row(yb.shape[1]), row(2 * D_MODEL),
                  full(wa), full(wb), full(wo)],
        out_specs=row(D_MODEL),
        out_shape=jax.ShapeDtypeStruct((n, D_MODEL), F32),
        compiler_params=_cparams("parallel"),
        name="merge",
    )(h, ya, yb, g, wa, wb, wo)


def _ple_kernel(h_ref, p_ref, lnp_ref, wg_ref, wp_ref, lnf_ref, o_ref):
    x = h_ref[...]
    u = _rms(x, lnp_ref[...]).astype(BF16)
    gate = jax.nn.sigmoid(jnp.dot(u, wg_ref[...], preferred_element_type=F32))
    proj = jnp.dot(p_ref[...].astype(BF16), wp_ref[...], preferred_element_type=F32)
    o_ref[...] = _rms(x + gate * proj, lnf_ref[...])


def _ple(h, p, lnp, wg, wp, lnf):
    n = h.shape[0]
    row = lambda w: pl.BlockSpec((MERGE_TM, w), lambda i: (i, 0))
    full = lambda a: pl.BlockSpec(a.shape, lambda i: (0, 0))
    return pl.pallas_call(
        _ple_kernel,
        grid=(n // MERGE_TM,),
        in_specs=[row(D_MODEL), row(D_PLE), full(lnp), full(wg), full(wp), full(lnf)],
        out_specs=row(D_MODEL),
        out_shape=jax.ShapeDtypeStruct((n, D_MODEL), F32),
        compiler_params=_cparams("parallel"),
        name="ple",
    )(h, p, lnp, wg, wp, lnf)


def kernel(x, p, ln_ffn1, w_ffn1_in, w_ffn1_out, ln_mix, w_in, a_sink, w_br_a, w_br_b, w_out,
           ln_ffn2, w_ffn2_in, w_ffn2_out, ln_ple, w_ple_gate, w_ple_proj, ln_final):
    bsz, t, d = x.shape
    assert p.shape[0] == 1, "the final norm is fused into the single layer's last kernel"
    n = bsz * t
    h = x.reshape(n, d)
    h = _ffn(h, ln_ffn1, w_ffn1_in[0].astype(BF16), w_ffn1_out[0].astype(BF16))
    aqt, ka, avt, iqt, bqt, ki, kb, vt, iw, g = _in_proj(h, ln_mix, _split_in_proj(w_in[0]), t // BLOCK)
    tok = lambda a: a.reshape(bsz, t, a.shape[-1])
    blk = lambda a: a.reshape(bsz, t // BLOCK, a.shape[-2], a.shape[-1])
    ya = _swa(a_sink[0], blk(aqt), tok(ka), blk(avt))
    yb = _dsa(blk(iqt), tok(iw), blk(bqt), tok(ki), tok(kb), blk(vt), _knorm(tok(kb)))
    h = _merge(h, ya.reshape(n, -1), yb.reshape(n, -1), g,
               w_br_a[0].astype(BF16), w_br_b[0].astype(BF16), w_out[0].astype(BF16))
    h = _ffn(h, ln_ffn2, w_ffn2_in[0].astype(BF16), w_ffn2_out[0].astype(BF16))
    h = _ple(h, p[0].reshape(n, -1), ln_ple, w_ple_gate[0].astype(BF16),
             w_ple_proj[0].astype(BF16), ln_final[None])
    return h.reshape(bsz, t, d)
```

```python
import functools

import numpy as np
import jax
import jax.numpy as jnp
from jax import lax
from jax.experimental import pallas as pl
from jax.experimental.pallas import tpu as pltpu

F32 = jnp.float32
BF16 = jnp.bfloat16
I32 = jnp.int32

D_MODEL = 1024
HEAD_DIM = 64
A_HEADS = 8
A_KV = 2
BLOCK = 128
B_HEADS = 8
IDX_HEADS = 8
TOPK = 256
D_FF = 2816
D_PLE = 256
EPS = 1e-6
NEG = -1e30

LANES = 128
VMEM_LIMIT = 48 * 1024 * 1024

_SLOPES = [float(np.float32(2.0) ** np.float32(-8.0 * i / (A_HEADS + B_HEADS)))
           for i in range(1, A_HEADS + B_HEADS + 1)]
SLOPES_A = _SLOPES[:A_HEADS]
SLOPES_B = _SLOPES[A_HEADS:]
LOG2E = float(np.log2(np.e))


def _bf16_terms(x, n=3):
    out = []
    for _ in range(n):
        t = float(np.float32(x).astype(jnp.bfloat16))
        out.append(t)
        x = x - t
    return out


SLOPE2_TERMS = [_bf16_terms(s * LOG2E) for s in SLOPES_B]


def _rms(x, g):
    return x * lax.rsqrt(jnp.mean(x * x, axis=-1, keepdims=True) + EPS) * g


def _cparams(*sem):
    return pltpu.CompilerParams(dimension_semantics=sem, vmem_limit_bytes=VMEM_LIMIT)


FFN_TM = 512


def _ffn_kernel(h_ref, ln_ref, wa_ref, wb_ref, wo_ref, o_ref):
    x = h_ref[...]
    xn = _rms(x, ln_ref[...]).astype(BF16)
    a = jnp.dot(xn, wa_ref[...], preferred_element_type=F32)
    b = jnp.dot(xn, wb_ref[...], preferred_element_type=F32)
    g = (a * jax.nn.sigmoid(a) * b).astype(BF16)
    o_ref[...] = x + 0.5 * jnp.dot(g, wo_ref[...], preferred_element_type=F32)


def _ffn(h, ln, w_in, w_out):
    n = h.shape[0]
    once = pl.Buffered(1)
    return pl.pallas_call(
        _ffn_kernel,
        grid=(n // FFN_TM,),
        in_specs=[
            pl.BlockSpec((FFN_TM, D_MODEL), lambda i: (i, 0)),
            pl.BlockSpec((1, D_MODEL), lambda i: (0, 0)),
            pl.BlockSpec((D_MODEL, D_FF), lambda i: (0, 0), pipeline_mode=once),
            pl.BlockSpec((D_MODEL, D_FF), lambda i: (0, 1), pipeline_mode=once),
            pl.BlockSpec((D_FF, D_MODEL), lambda i: (0, 0), pipeline_mode=once),
        ],
        out_specs=pl.BlockSpec((FFN_TM, D_MODEL), lambda i: (i, 0)),
        out_shape=jax.ShapeDtypeStruct((n, D_MODEL), F32),
        compiler_params=_cparams("parallel"),
        name="ffn",
    )(h, ln, w_in, w_in, w_out)


PROJ_TM = 256


def _inproj_kernel(blocks_per_seq, h_ref, ln_ref, waq, wka, wav, wiq, wbq, wki, wkb, wv, wiw, wg,
                   o_aqt, o_ka, o_avt, o_iqt, o_bqt, o_ki, o_kb, o_vt, o_iw, o_g):
    u = _rms(h_ref[...], ln_ref[...]).astype(BF16)

    def mm(w):
        return jnp.dot(u, w[...], preferred_element_type=F32)

    o_ka[...] = mm(wka).astype(BF16)
    o_iw[...] = mm(wiw)
    o_g[...] = jax.nn.sigmoid(mm(wg))

    row = lax.broadcasted_iota(I32, (PROJ_TM, LANES), 0)
    lane = lax.broadcasted_iota(I32, (PROJ_TM, LANES), 1)
    block = ((pl.program_id(0) * PROJ_TM + row) // BLOCK) % blocks_per_seq
    pos = jnp.where(lane < POS_LANE + 3, (row % BLOCK).astype(F32),
                    jnp.where(lane < POS_LANE + 6, block.astype(F32), 1.0))
    pos = jnp.where((lane >= POS_LANE) & (lane < POS_LANE + 9), pos, 0.0)
    o_ki[...] = mm(wki).astype(BF16)
    o_kb[...] = (mm(wkb) + pos).astype(BF16)
    v1 = jnp.where(lane == HEAD_DIM, 1.0, mm(wv))
    aq = mm(waq)
    av = mm(wav)
    iq = mm(wiq)
    bq = mm(wbq)
    row64 = lax.broadcasted_iota(I32, (HEAD_DIM, BLOCK), 0)
    zeros64 = jnp.zeros((HEAD_DIM, BLOCK), F32)
    for blk in range(PROJ_TM // BLOCK):
        tok = slice(blk * BLOCK, (blk + 1) * BLOCK)
        o_vt[blk] = v1[tok].T.astype(BF16)
        o_avt[blk] = av[tok].T.astype(BF16)
        for j in range(B_HEADS // 2):
            aq_t = aq[tok, j * LANES:(j + 1) * LANES].T
            iq_t = iq[tok, j * LANES:(j + 1) * LANES].T
            bq_t = bq[tok, j * LANES:(j + 1) * LANES].T
            for hh in range(2):
                h = 2 * j + hh
                cols = slice(h * BLOCK, (h + 1) * BLOCK)
                part = slice(hh * HEAD_DIM, (hh + 1) * HEAD_DIM)
                o_aqt[blk, :, cols] = jnp.concatenate([aq_t[part], zeros64], axis=0).astype(BF16)
                o_iqt[blk, :, cols] = jnp.concatenate([iq_t[part], zeros64], axis=0).astype(BF16)
                slope_rows = zeros64
                for r, term in enumerate(SLOPE2_TERMS[h]):
                    slope_rows = jnp.where(row64 == r, term, slope_rows)
                    slope_rows = jnp.where(row64 == r + 3, term * BLOCK, slope_rows)
                o_bqt[blk, :, cols] = jnp.concatenate([bq_t[part], slope_rows], axis=0).astype(BF16)


def _split_in_proj(w):
    scale = HEAD_DIM ** -0.5
    aq, ak, av = w[:, 0:512] * scale, w[:, 512:640], w[:, 640:768]
    bq, bk, bv = w[:, 768:1280] * (scale * LOG2E), w[:, 1280:1344], w[:, 1344:1408]
    iq, ik, iw = w[:, 1408:1920] * scale, w[:, 1920:1984], w[:, 1984:1992]
    g = w[:, 1992:4040]
    z64 = jnp.zeros((w.shape[0], HEAD_DIM), w.dtype)
    return dict(
        waq=aq.astype(BF16),
        wka=jnp.concatenate([ak[:, :HEAD_DIM], z64, ak[:, HEAD_DIM:], z64], axis=1).astype(BF16),
        wav=av.astype(BF16),
        wiq=iq.astype(BF16),
        wbq=bq.astype(BF16),
        wki=jnp.concatenate([ik, z64], axis=1).astype(BF16),
        wkb=jnp.concatenate([bk, z64], axis=1).astype(BF16),
        wv=jnp.concatenate([bv, z64], axis=1).astype(BF16),
        wiw=jnp.pad(iw, ((0, 0), (0, LANES - IDX_HEADS))).astype(BF16),
        wg=g.astype(BF16),
    )


def _in_proj(h, ln, ws, blocks_per_seq):
    n = h.shape[0]
    blocks = PROJ_TM // BLOCK
    names = ["waq", "wka", "wav", "wiq", "wbq", "wki", "wkb", "wv", "wiw", "wg"]
    rows = lambda wd, dt: (pl.BlockSpec((PROJ_TM, wd), lambda i: (i, 0)), jax.ShapeDtypeStruct((n, wd), dt))
    tiles = lambda r, c: (pl.BlockSpec((blocks, r, c), lambda i: (i, 0, 0)),
                          jax.ShapeDtypeStruct((n // BLOCK, r, c), BF16))
    outs = [tiles(LANES, A_HEADS * BLOCK), rows(A_KV * LANES, BF16), tiles(LANES, BLOCK),
            tiles(LANES, IDX_HEADS * BLOCK), tiles(LANES, B_HEADS * BLOCK),
            rows(LANES, BF16), rows(LANES, BF16), tiles(LANES, BLOCK),
            rows(LANES, F32), rows(2 * D_MODEL, F32)]
    return pl.pallas_call(
        functools.partial(_inproj_kernel, blocks_per_seq),
        grid=(n // PROJ_TM,),
        in_specs=[pl.BlockSpec((PROJ_TM, D_MODEL), lambda i: (i, 0)),
                  pl.BlockSpec((1, D_MODEL), lambda i: (0, 0))]
        + [pl.BlockSpec((D_MODEL, ws[k].shape[1]), lambda i: (0, 0)) for k in names],
        out_specs=[o[0] for o in outs],
        out_shape=[o[1] for o in outs],
        compiler_params=_cparams("parallel"),
        name="in_proj",
    )(h, ln, *[ws[k] for k in names])


def _swa_kernel(sink_ref, qt_ref, kp_ref, kc_ref, vtp_ref, vtc_ref, o_ref):
    n = pl.program_id(1)
    qt = qt_ref[0, 0]
    j = lax.broadcasted_iota(I32, (BLOCK, BLOCK), 0)
    i = lax.broadcasted_iota(I32, (BLOCK, BLOCK), 1)
    lower = j <= i
    dcur = (i - j).astype(F32)
    dprev = (i - j + BLOCK).astype(F32)
    prev_bias = jnp.where(n > 0, 0.0, NEG).astype(F32)
    per_group = A_HEADS // A_KV
    outs = []
    for g in range(A_KV):
        k2 = jnp.concatenate([kp_ref[0, :, g * LANES:(g + 1) * LANES],
                              kc_ref[0, :, g * LANES:(g + 1) * LANES]], axis=0)
        s2 = jnp.dot(k2, qt[:, g * per_group * BLOCK:(g + 1) * per_group * BLOCK],
                     preferred_element_type=F32)
        ps = []
        for r in range(per_group):
            h = g * per_group + r
            sp = s2[0:BLOCK, r * BLOCK:(r + 1) * BLOCK]
            sc = s2[BLOCK:2 * BLOCK, r * BLOCK:(r + 1) * BLOCK]
            s = jnp.where(lower, sc - SLOPES_A[h] * dcur, sp - SLOPES_A[h] * dprev + prev_bias)
            sink = sink_ref[h]
            m = jnp.maximum(jnp.max(s, axis=0, keepdims=True), sink)
            e = jnp.exp(s - m)
            p = e / (jnp.sum(e, axis=0, keepdims=True) + jnp.exp(sink - m))
            ps.append(jnp.concatenate([jnp.where(lower, 0.0, p), jnp.where(lower, p, 0.0)],
                                      axis=0).astype(BF16))
        rows = slice(g * HEAD_DIM, (g + 1) * HEAD_DIM)
        vt2 = jnp.concatenate([vtp_ref[0, 0, rows, :], vtc_ref[0, 0, rows, :]], axis=1)
        o_t = jnp.dot(vt2, jnp.concatenate(ps, axis=1), preferred_element_type=F32)
        for r in range(0, per_group, 2):
            pair = jnp.concatenate([o_t[:, r * BLOCK:(r + 1) * BLOCK],
                                    o_t[:, (r + 1) * BLOCK:(r + 2) * BLOCK]], axis=0)
            outs.append(pair.T)
    o_ref[0] = jnp.concatenate(outs, axis=1).astype(BF16)


def _swa(sink, aqt, ka, avt):
    bsz, nb = aqt.shape[:2]
    t = nb * BLOCK
    prev = lambda b, n: (b, jnp.maximum(n - 1, 0), 0)
    prev4 = lambda b, n: (b, jnp.maximum(n - 1, 0), 0, 0)
    return pl.pallas_call(
        _swa_kernel,
        grid=(bsz, nb),
        in_specs=[
            pl.BlockSpec(memory_space=pltpu.SMEM),
            pl.BlockSpec((1, 1, LANES, A_HEADS * BLOCK), lambda b, n: (b, n, 0, 0)),
            pl.BlockSpec((1, BLOCK, A_KV * LANES), prev),
            pl.BlockSpec((1, BLOCK, A_KV * LANES), lambda b, n: (b, n, 0)),
            pl.BlockSpec((1, 1, LANES, BLOCK), prev4),
            pl.BlockSpec((1, 1, LANES, BLOCK), lambda b, n: (b, n, 0, 0)),
        ],
        out_specs=pl.BlockSpec((1, BLOCK, A_HEADS * HEAD_DIM), lambda b, n: (b, n, 0)),
        out_shape=jax.ShapeDtypeStruct((bsz, t, A_HEADS * HEAD_DIM), BF16),
        compiler_params=_cparams("parallel", "arbitrary"),
        name="swa",
    )(sink, aqt, ka, ka, avt, avt)


CK = 256
CHUNK_UNROLL = 4
SUB = 128
POS_LANE = HEAD_DIM
PIPE = 4
ACC_ROWS = 80
TINY = 1.1754944e-38
FREE_STEPS = 14
MIN_DENOMINATOR = 2.0 ** -100


def _tree_sum(xs):
    while len(xs) > 1:
        xs = [a + b for a, b in zip(xs[0::2], xs[1::2])]
    return xs[0]


def _for_chunks(nck, body, carry):
    def group(i, carry):
        for u in range(CHUNK_UNROLL):
            carry = body(i * CHUNK_UNROLL + u, carry)
        return carry
    return lax.fori_loop(0, (nck + CHUNK_UNROLL - 1) // CHUNK_UNROLL, group, carry)


def _dsa_kernel(iqt_ref, iw_ref, bqt_ref, ki_ref, kb_ref, vt_ref, kn_ref, yb_ref, sc_ref, thr_ref, s_ref, p_ref,
                acc_ref):
    n = pl.program_id(1)
    nck = n // (CK // BLOCK) + 1
    tq = n * BLOCK + lax.broadcasted_iota(I32, (CK, LANES), 1)
    rel = lax.broadcasted_iota(I32, (CK, LANES), 0)

    qi_t = iqt_ref[0, 0]
    iw_t = (iw_ref[0] * (IDX_HEADS ** -0.5)).T

    def score_chunk(c, carry):
        smin, smax = carry
        start = pl.multiple_of(c * CK, CK)
        s = jnp.dot(ki_ref[0, pl.ds(start, CK), :], qi_t, preferred_element_type=F32)
        acc = _tree_sum([jnp.maximum(s[:, h * LANES:(h + 1) * LANES], 0.0) * iw_t[h:h + 1, :]
                         for h in range(IDX_HEADS)])
        causal = c * CK + rel <= tq
        sc_ref[c] = jnp.where(causal, acc, NEG)
        smax = jnp.maximum(smax, jnp.max(jnp.where(causal, acc, NEG), axis=0, keepdims=True))
        smin = jnp.minimum(smin, jnp.min(jnp.where(causal, acc, -NEG), axis=0, keepdims=True))
        return smin, smax

    smin, smax = _for_chunks(nck, score_chunk,
                             (jnp.full((1, LANES), -NEG, F32), jnp.full((1, LANES), NEG, F32)))

    thr_ref[...] = jnp.full(thr_ref.shape, NEG, F32)

    @pl.when(n >= 2)
    def _():
        def count_ge(probe):
            def body(c, cnt):
                hit = jnp.where(sc_ref[c] >= probe, 1.0, 0.0)
                return cnt + jnp.sum(hit.reshape(CK // 32, 4, 8, LANES), axis=0)
            cnt = _for_chunks(nck, body, jnp.zeros((4, 8, LANES), F32))
            return jnp.sum(cnt.reshape(32, LANES), axis=0, keepdims=True)

        def probe_of(lo, hi):
            p = 0.5 * lo + 0.5 * hi
            p = jnp.where((lo < 0.0) & (hi > 0.0), 0.0, p)
            p = jnp.where((lo == 0.0) & (hi > TINY), TINY, p)
            return jnp.where((hi == 0.0) & (lo < -TINY), -TINY, p)

        def cond(st):
            lo, hi, c_lo, c_hi = st
            p = probe_of(lo, hi)
            done = (c_lo == TOPK) | (hi - lo <= TINY) | ~((p > lo) & (p < hi))
            return jnp.max(jnp.where(done, 0.0, 1.0)) > 0.0

        def step(st):
            lo, hi, c_lo, c_hi = st
            p = probe_of(lo, hi)
            c = count_ge(p)
            ge = c >= TOPK
            return (jnp.where(ge, p, lo), jnp.where(ge, hi, p),
                    jnp.where(ge, c, c_lo), jnp.where(ge, c_hi, c))

        n_causal = (tq[0:1, :] + 1).astype(F32)
        st0 = (smin, smax + (jnp.abs(smax) * 1e-6 + TINY), n_causal, jnp.zeros((1, LANES), F32))
        st1 = lax.fori_loop(0, FREE_STEPS, lambda _, st: step(st), st0)
        lo, hi, c_lo, c_hi = lax.while_loop(cond, lambda st: step(step(st)), st1)
        thr_ref[...] = jnp.broadcast_to(lo, thr_ref.shape)

        tied = c_lo > TOPK

        @pl.when(jnp.max(jnp.where(tied, 1.0, 0.0)) > 0.0)
        def _():
            keep = TOPK - c_hi
            tri = jnp.where(lax.broadcasted_iota(I32, (CK, CK), 0) >= lax.broadcasted_iota(I32, (CK, CK), 1),
                            1.0, 0.0).astype(BF16)

            def drop(c, before):
                x = sc_ref[c]
                eq = x == lo
                prefix = before + jnp.dot(tri, jnp.where(eq, 1.0, 0.0).astype(BF16),
                                          preferred_element_type=F32)
                sc_ref[c] = jnp.where(eq & tied & (prefix > keep), NEG, x)
                return prefix[CK - 1:CK, :]

            lax.fori_loop(0, nck, drop, jnp.zeros((1, LANES), F32))

    thr = thr_ref[0:1, :]
    rel_s = lax.broadcasted_iota(I32, (SUB, LANES), 0)
    tq_s = n * BLOCK + lax.broadcasted_iota(I32, (SUB, LANES), 1)
    last = pl.num_programs(1) - 1
    q2_t = bqt_ref[0, 0]

    def key_rows(t):
        return kb_ref[0, pl.ds(pl.multiple_of(jnp.minimum(t, last) * SUB, SUB), SUB), :]

    def value_rows(t):
        return vt_ref[0, jnp.clip(t, 0, last), 0:ACC_ROWS, :]

    def select_bias(t):
        tc = jnp.minimum(t, n)
        rows = pl.ds(pl.multiple_of((tc % 2) * SUB, SUB), SUB)
        sel = (sc_ref[tc // 2, rows, :] >= thr) & (t * SUB + rel_s <= tq_s)
        return jnp.where(sel, 0.0, NEG)

    qf = q2_t[0:HEAD_DIM].astype(F32)
    q_norm2 = jnp.sum(qf * qf, axis=0, keepdims=True)
    k_norm2 = jnp.concatenate([kn_ref[0, 0:1, :]] * B_HEADS, axis=1)
    tq_row = jnp.concatenate([tq_s[0:1].astype(F32)] * B_HEADS, axis=1)
    slope_row = jnp.concatenate([jnp.full((1, LANES), float(np.float32(sum(t))), F32) for t in SLOPE2_TERMS],
                                axis=1)
    offset = jnp.sqrt(q_norm2 * k_norm2) * (1.0 + 2.0 ** -6) + 2.0 ** -6 + slope_row * tq_row
    terms, rest = [], -offset
    for _ in range(3):
        terms.append(rest.astype(BF16).astype(F32))
        rest = rest - terms[-1]
    row_q = lax.broadcasted_iota(I32, (LANES, B_HEADS * LANES), 0)
    q2_off = q2_t.astype(F32)
    for r, term in enumerate(terms):
        q2_off = jnp.where(row_q == POS_LANE + 6 + r, term, q2_off)
    q2_off = q2_off.astype(BF16)

    def stage_logits(t, k):
        s_ref[k] = jnp.dot(key_rows(t), q2_off, preferred_element_type=F32)

    def probabilities(t, k):
        negb = select_bias(t)
        for h in range(B_HEADS):
            cols = slice(h * LANES, (h + 1) * LANES)
            p_ref[k, :, cols] = jnp.exp2(s_ref[k, :, cols] + negb).astype(BF16)

    def pv_accumulate(t, k):
        values = jnp.concatenate([value_rows(t), value_rows(t + 1)], axis=1)
        probs = jnp.concatenate([p_ref[k], p_ref[k + 1]], axis=0)
        acc_ref[...] += jnp.dot(values, probs, preferred_element_type=F32)

    def group(g, carry):
        for k in range(0, PIPE, 2):
            pv_accumulate((g - 1) * PIPE + k, k)
            for kk in (k, k + 1):
                probabilities(g * PIPE + kk, kk)
                stage_logits((g + 1) * PIPE + kk, kk)
        return carry

    ngroups = n // PIPE + 1
    acc_ref[...] = jnp.zeros_like(acc_ref)
    for k in range(PIPE):
        stage_logits(k, k)
    p_ref[...] = jnp.zeros(p_ref.shape, BF16)
    lax.fori_loop(0, ngroups, group, 0)
    for k in range(0, PIPE, 2):
        pv_accumulate((ngroups - 1) * PIPE + k, k)

    @pl.when(jnp.logical_not(jnp.min(acc_ref[HEAD_DIM:HEAD_DIM + 1, :]) > MIN_DENOMINATOR))
    def _():
        acc_ref[...] = jnp.zeros_like(acc_ref)

        def exact_step(t, m):
            s = jnp.dot(key_rows(t), q2_t, preferred_element_type=F32)
            negb = select_bias(t)
            ps, alphas, m_out = [], [], []
            for h in range(B_HEADS):
                l = s[:, h * LANES:(h + 1) * LANES] + negb
                m_new = jnp.maximum(m[h], jnp.max(l, axis=0, keepdims=True))
                alphas.append(jnp.exp2(m[h] - m_new))
                ps.append(jnp.exp2(l - m_new).astype(BF16))
                m_out.append(m_new)
            pv = jnp.dot(value_rows(t), jnp.concatenate(ps, axis=1), preferred_element_type=F32)
            acc_ref[...] = acc_ref[...] * jnp.concatenate(alphas, axis=1) + pv
            return m_out

        lax.fori_loop(0, n + 1, exact_step, [jnp.full((1, LANES), NEG, F32) for _ in range(B_HEADS)])

    outs = []
    pad = jnp.zeros((LANES - HEAD_DIM, LANES), F32)
    for h in range(B_HEADS):
        blk = acc_ref[:, h * LANES:(h + 1) * LANES]
        o_t = blk[0:HEAD_DIM] / blk[HEAD_DIM:HEAD_DIM + 1]
        outs.append(jnp.concatenate([o_t, pad], axis=0).T[:, 0:HEAD_DIM])
    yb_ref[0] = jnp.concatenate(outs, axis=1).astype(BF16)


def _knorm_kernel(kb_ref, o_ref):
    k = kb_ref[0].astype(F32)
    lane = lax.broadcasted_iota(I32, k.shape, 1)
    sq = jnp.where(lane < HEAD_DIM, k * k, 0.0)
    o_ref[0] = jnp.full(o_ref.shape[1:], jnp.max(jnp.sum(sq, axis=1, keepdims=True)), F32)


def _knorm(kb):
    bsz, t, _ = kb.shape
    return pl.pallas_call(
        _knorm_kernel,
        grid=(bsz,),
        in_specs=[pl.BlockSpec((1, t, LANES), lambda b: (b, 0, 0))],
        out_specs=pl.BlockSpec((1, 8, LANES), lambda b: (b, 0, 0)),
        out_shape=jax.ShapeDtypeStruct((bsz, 8, LANES), F32),
        compiler_params=_cparams("parallel"),
        name="knorm",
    )(kb)


def _dsa(iqt, iw, bqt, ki, kb, vt, kn):
    bsz, t, _ = iw.shape
    nb = t // BLOCK
    assert t % (CK * CHUNK_UNROLL) == 0, "rounded-up chunk loops must stay inside the sequence"
    return pl.pallas_call(
        _dsa_kernel,
        grid=(bsz, nb),
        in_specs=[
            pl.BlockSpec((1, 1, LANES, IDX_HEADS * BLOCK), lambda b, n: (b, n, 0, 0)),
            pl.BlockSpec((1, BLOCK, LANES), lambda b, n: (b, n, 0)),
            pl.BlockSpec((1, 1, LANES, B_HEADS * BLOCK), lambda b, n: (b, n, 0, 0)),
            pl.BlockSpec((1, t, LANES), lambda b, n: (b, 0, 0)),
            pl.BlockSpec((1, t, LANES), lambda b, n: (b, 0, 0)),
            pl.BlockSpec((1, nb, LANES, BLOCK), lambda b, n: (b, 0, 0, 0)),
            pl.BlockSpec((1, 8, LANES), lambda b, n: (b, 0, 0)),
        ],
        out_specs=pl.BlockSpec((1, BLOCK, B_HEADS * HEAD_DIM), lambda b, n: (b, n, 0)),
        out_shape=jax.ShapeDtypeStruct((bsz, t, B_HEADS * HEAD_DIM), BF16),
        scratch_shapes=[
            pltpu.VMEM((t // CK, CK, LANES), F32),
            pltpu.VMEM((8, LANES), F32),
            pltpu.VMEM((PIPE, SUB, B_HEADS * LANES), F32),
            pltpu.VMEM((PIPE, SUB, B_HEADS * LANES), BF16),
            pltpu.VMEM((ACC_ROWS, B_HEADS * LANES), F32),
        ],
        compiler_params=_cparams("parallel", "arbitrary"),
        name="dsa",
    )(iqt, iw, bqt, ki, kb, vt, kn)


MERGE_TM = 512


def _merge_kernel(h_ref, ya_ref, yb_ref, g_ref, wa_ref, wb_ref, wo_ref, o_ref):
    ta = jnp.dot(ya_ref[...], wa_ref[...], preferred_element_type=F32)
    tb = jnp.dot(yb_ref[...], wb_ref[...], preferred_element_type=F32)
    g = g_ref[...]
    mix = (g[:, :D_MODEL] * ta + g[:, D_MODEL:] * tb).astype(BF16)
    o_ref[...] = h_ref[...] + jnp.dot(mix, wo_ref[...], preferred_element_type=F32)


def _merge(h, ya, yb, g, wa, wb, wo):
    n = h.shape[0]
    row = lambda w: pl.BlockSpec((MERGE_TM, w), lambda i: (i, 0))
    full = lambda a: pl.BlockSpec(a.shape, lambda i: (0, 0))
    return pl.pallas_call(
        _merge_kernel,
        grid=(n // MERGE_TM,),
        in_specs=[row(D_MODEL), row(ya.shape[1]), row(yb.shape[1]), row(2 * D_MODEL),
                  full(wa), full(wb), full(wo)],
        out_specs=row(D_MODEL),
        out_shape=jax.ShapeDtypeStruct((n, D_MODEL), F32),
        compiler_params=_cparams("parallel"),
        name="merge",
    )(h, ya, yb, g, wa, wb, wo)


def _ple_kernel(h_ref, p_ref, lnp_ref, wg_ref, wp_ref, lnf_ref, o_ref):
    x = h_ref[...]
    u = _rms(x, lnp_ref[...]).astype(BF16)
    gate = jax.nn.sigmoid(jnp.dot(u, wg_ref[...], preferred_element_type=F32))
    proj = jnp.dot(p_ref[...].astype(BF16), wp_ref[...], preferred_element_type=F32)
    o_ref[...] = _rms(x + gate * proj, lnf_ref[...])


def _ple(h, p, lnp, wg, wp, lnf):
    n = h.shape[0]
    row = lambda w: pl.BlockSpec((MERGE_TM, w), lambda i: (i, 0))
    full = lambda a: pl.BlockSpec(a.shape, lambda i: (0, 0))
    return pl.pallas_call(
        _ple_kernel,
        grid=(n // MERGE_TM,),
        in_specs=[row(D_MODEL), row(D_PLE), full(lnp), full(wg), full(wp), full(lnf)],
        out_specs=row(D_MODEL),
        out_shape=jax.ShapeDtypeStruct((n, D_MODEL), F32),
        compiler_params=_cparams("parallel"),
        name="ple",
    )(h, p, lnp, wg, wp, lnf)


def kernel(x, p, ln_ffn1, w_ffn1_in, w_ffn1_out, ln_mix, w_in, a_sink, w_br_a, w_br_b, w_out,
           ln_ffn2, w_ffn2_in, w_ffn2_out, ln_ple, w_ple_gate, w_ple_proj, ln_final):
    bsz, t, d = x.shape
    assert p.shape[0] == 1, "the final norm is fused into the single layer's last kernel"
    n = bsz * t
    h = x.reshape(n, d)
    h = _ffn(h, ln_ffn1, w_ffn1_in[0].astype(BF16), w_ffn1_out[0].astype(BF16))
    aqt, ka, avt, iqt, bqt, ki, kb, vt, iw, g = _in_proj(h, ln_mix, _split_in_proj(w_in[0]), t // BLOCK)
    tok = lambda a: a.reshape(bsz, t, a.shape[-1])
    blk = lambda a: a.reshape(bsz, t // BLOCK, a.shape[-2], a.shape[-1])
    ya = _swa(a_sink[0], blk(aqt), tok(ka), blk(avt))
    yb = _dsa(blk(iqt), tok(iw), blk(bqt), tok(ki), tok(kb), blk(vt), _knorm(tok(kb)))
    h = _merge(h, ya.reshape(n, -1), yb.reshape(n, -1), g,
               w_br_a[0].astype(BF16), w_br_b[0].astype(BF16), w_out[0].astype(BF16))
    h = _ffn(h, ln_ffn2, w_ffn2_in[0].astype(BF16), w_ffn2_out[0].astype(BF16))
    h = _ple(h, p[0].reshape(n, -1), ln_ple, w_ple_gate[0].astype(BF16),
             w_ple_proj[0].astype(BF16), ln_final[None])
    return h.reshape(bsz, t, d)
```

```python
import functools

import numpy as np
import jax
import jax.numpy as jnp
from jax import lax
from jax.experimental import pallas as pl
from jax.experimental.pallas import tpu as pltpu

F32 = jnp.float32
BF16 = jnp.bfloat16
I32 = jnp.int32

D_MODEL = 1024
HEAD_DIM = 64
A_HEADS = 8
A_KV = 2
BLOCK = 128
B_HEADS = 8
IDX_HEADS = 8
TOPK = 256
D_FF = 2816
D_PLE = 256
EPS = 1e-6
NEG = -1e30

LANES = 128
VMEM_LIMIT = 48 * 1024 * 1024

_SLOPES = [float(np.float32(2.0) ** np.float32(-8.0 * i / (A_HEADS + B_HEADS)))
           for i in range(1, A_HEADS + B_HEADS + 1)]
SLOPES_A = _SLOPES[:A_HEADS]
SLOPES_B = _SLOPES[A_HEADS:]
LOG2E = float(np.log2(np.e))


def _bf16_terms(x, n=3):
    out = []
    for _ in range(n):
        t = float(np.float32(x).astype(jnp.bfloat16))
        out.append(t)
        x = x - t
    return out


SLOPE2_TERMS = [_bf16_terms(s * LOG2E) for s in SLOPES_B]


def _rms(x, g):
    return x * lax.rsqrt(jnp.mean(x * x, axis=-1, keepdims=True) + EPS) * g


def _cparams(*sem):
    return pltpu.CompilerParams(dimension_semantics=sem, vmem_limit_bytes=VMEM_LIMIT)


FFN_TM = 512


def _ffn_kernel(h_ref, ln_ref, wa_ref, wb_ref, wo_ref, o_ref):
    x = h_ref[...]
    xn = _rms(x, ln_ref[...]).astype(BF16)
    a = jnp.dot(xn, wa_ref[...], preferred_element_type=F32)
    b = jnp.dot(xn, wb_ref[...], preferred_element_type=F32)
    g = (a * jax.nn.sigmoid(a) * b).astype(BF16)
    o_ref[...] = x + 0.5 * jnp.dot(g, wo_ref[...], preferred_element_type=F32)


def _ffn(h, ln, w_in, w_out):
    n = h.shape[0]
    once = pl.Buffered(1)
    return pl.pallas_call(
        _ffn_kernel,
        grid=(n // FFN_TM,),
        in_specs=[
            pl.BlockSpec((FFN_TM, D_MODEL), lambda i: (i, 0)),
            pl.BlockSpec((1, D_MODEL), lambda i: (0, 0)),
            pl.BlockSpec((D_MODEL, D_FF), lambda i: (0, 0), pipeline_mode=once),
            pl.BlockSpec((D_MODEL, D_FF), lambda i: (0, 1), pipeline_mode=once),
            pl.BlockSpec((D_FF, D_MODEL), lambda i: (0, 0), pipeline_mode=once),
        ],
        out_specs=pl.BlockSpec((FFN_TM, D_MODEL), lambda i: (i, 0)),
        out_shape=jax.ShapeDtypeStruct((n, D_MODEL), F32),
        compiler_params=_cparams("parallel"),
        name="ffn",
    )(h, ln, w_in, w_in, w_out)


PROJ_TM = 256


def _inproj_kernel(blocks_per_seq, h_ref, ln_ref, waq, wka, wav, wiq, wbq, wki, wkb, wv, wiw, wg,
                   o_aqt, o_ka, o_avt, o_iqt, o_bqt, o_ki, o_kb, o_vt, o_iw, o_g):
    u = _rms(h_ref[...], ln_ref[...]).astype(BF16)

    def mm(w):
        return jnp.dot(u, w[...], preferred_element_type=F32)

    o_ka[...] = mm(wka).astype(BF16)
    o_iw[...] = mm(wiw)
    o_g[...] = jax.nn.sigmoid(mm(wg))

    row = lax.broadcasted_iota(I32, (PROJ_TM, LANES), 0)
    lane = lax.broadcasted_iota(I32, (PROJ_TM, LANES), 1)
    block = ((pl.program_id(0) * PROJ_TM + row) // BLOCK) % blocks_per_seq
    pos = jnp.where(lane < POS_LANE + 3, (row % BLOCK).astype(F32),
                    jnp.where(lane < POS_LANE + 6, block.astype(F32), 1.0))
    pos = jnp.where((lane >= POS_LANE) & (lane < POS_LANE + 9), pos, 0.0)
    o_ki[...] = mm(wki).astype(BF16)
    o_kb[...] = (mm(wkb) + pos).astype(BF16)
    v1 = jnp.where(lane == HEAD_DIM, 1.0, mm(wv))
    aq = mm(waq)
    av = mm(wav)
    iq = mm(wiq)
    bq = mm(wbq)
    row64 = lax.broadcasted_iota(I32, (HEAD_DIM, BLOCK), 0)
    zeros64 = jnp.zeros((HEAD_DIM, BLOCK), F32)
    for blk in range(PROJ_TM // BLOCK):
        tok = slice(blk * BLOCK, (blk + 1) * BLOCK)
        o_vt[blk] = v1[tok].T.astype(BF16)
        o_avt[blk] = av[tok].T.astype(BF16)
        for j in range(B_HEADS // 2):
            aq_t = aq[tok, j * LANES:(j + 1) * LANES].T
            iq_t = iq[tok, j * LANES:(j + 1) * LANES].T
            bq_t = bq[tok, j * LANES:(j + 1) * LANES].T
            for hh in range(2):
                h = 2 * j + hh
                cols = slice(h * BLOCK, (h + 1) * BLOCK)
                part = slice(hh * HEAD_DIM, (hh + 1) * HEAD_DIM)
                o_aqt[blk, :, cols] = jnp.concatenate([aq_t[part], zeros64], axis=0).astype(BF16)
                o_iqt[blk, :, cols] = jnp.concatenate([iq_t[part], zeros64], axis=0).astype(BF16)
                slope_rows = zeros64
                for r, term in enumerate(SLOPE2_TERMS[h]):
                    slope_rows = jnp.where(row64 == r, term, slope_rows)
                    slope_rows = jnp.where(row64 == r + 3, term * BLOCK, slope_rows)
                o_bqt[blk, :, cols] = jnp.concatenate([bq_t[part], slope_rows], axis=0).astype(BF16)


def _split_in_proj(w):
    scale = HEAD_DIM ** -0.5
    aq, ak, av = w[:, 0:512] * scale, w[:, 512:640], w[:, 640:768]
    bq, bk, bv = w[:, 768:1280] * (scale * LOG2E), w[:, 1280:1344], w[:, 1344:1408]
    iq, ik, iw = w[:, 1408:1920] * scale, w[:, 1920:1984], w[:, 1984:1992]
    g = w[:, 1992:4040]
    z64 = jnp.zeros((w.shape[0], HEAD_DIM), w.dtype)
    return dict(
        waq=aq.astype(BF16),
        wka=jnp.concatenate([ak[:, :HEAD_DIM], z64, ak[:, HEAD_DIM:], z64], axis=1).astype(BF16),
        wav=av.astype(BF16),
        wiq=iq.astype(BF16),
        wbq=bq.astype(BF16),
        wki=jnp.concatenate([ik, z64], axis=1).astype(BF16),
        wkb=jnp.concatenate([bk, z64], axis=1).astype(BF16),
        wv=jnp.concatenate([bv, z64], axis=1).astype(BF16),
        wiw=jnp.pad(iw, ((0, 0), (0, LANES - IDX_HEADS))).astype(BF16),
        wg=g.astype(BF16),
    )


def _in_proj(h, ln, ws, blocks_per_seq):
    n = h.shape[0]
    blocks = PROJ_TM // BLOCK
    names = ["waq", "wka", "wav", "wiq", "wbq", "wki", "wkb", "wv", "wiw", "wg"]
    rows = lambda wd, dt: (pl.BlockSpec((PROJ_TM, wd), lambda i: (i, 0)), jax.ShapeDtypeStruct((n, wd), dt))
    tiles = lambda r, c: (pl.BlockSpec((blocks, r, c), lambda i: (i, 0, 0)),
                          jax.ShapeDtypeStruct((n // BLOCK, r, c), BF16))
    outs = [tiles(LANES, A_HEADS * BLOCK), rows(A_KV * LANES, BF16), tiles(LANES, BLOCK),
            tiles(LANES, IDX_HEADS * BLOCK), tiles(LANES, B_HEADS * BLOCK),
            rows(LANES, BF16), rows(LANES, BF16), tiles(LANES, BLOCK),
            rows(LANES, F32), rows(2 * D_MODEL, F32)]
    return pl.pallas_call(
        functools.partial(_inproj_kernel, blocks_per_seq),
        grid=(n // PROJ_TM,),
        in_specs=[pl.BlockSpec((PROJ_TM, D_MODEL), lambda i: (i, 0)),
                  pl.BlockSpec((1, D_MODEL), lambda i: (0, 0))]
        + [pl.BlockSpec((D_MODEL, ws[k].shape[1]), lambda i: (0, 0)) for k in names],
        out_specs=[o[0] for o in outs],
        out_shape=[o[1] for o in outs],
        compiler_params=_cparams("parallel"),
        name="in_proj",
    )(h, ln, *[ws[k] for k in names])


SWA_BLOCKS = 2


def _swa_block(sink_ref, first, qt, kp, kc, vtp, vtc):
    j = lax.broadcasted_iota(I32, (BLOCK, BLOCK), 0)
    i = lax.broadcasted_iota(I32, (BLOCK, BLOCK), 1)
    lower = j <= i
    dcur = (i - j).astype(F32)
    dprev = (i - j + BLOCK).astype(F32)
    prev_bias = jnp.where(first, NEG, 0.0).astype(F32)
    per_group = A_HEADS // A_KV
    outs = []
    for g in range(A_KV):
        k2 = jnp.concatenate([kp[:, g * LANES:(g + 1) * LANES], kc[:, g * LANES:(g + 1) * LANES]], axis=0)
        s2 = jnp.dot(k2, qt[:, g * per_group * BLOCK:(g + 1) * per_group * BLOCK],
                     preferred_element_type=F32)
        ps = []
        for r in range(per_group):
            h = g * per_group + r
            sp = s2[0:BLOCK, r * BLOCK:(r + 1) * BLOCK]
            sc = s2[BLOCK:2 * BLOCK, r * BLOCK:(r + 1) * BLOCK]
            s = jnp.where(lower, sc - SLOPES_A[h] * dcur, sp - SLOPES_A[h] * dprev + prev_bias)
            sink = sink_ref[h]
            m = jnp.maximum(jnp.max(s, axis=0, keepdims=True), sink)
            e = jnp.exp(s - m)
            p = e / (jnp.sum(e, axis=0, keepdims=True) + jnp.exp(sink - m))
            ps.append(jnp.concatenate([jnp.where(lower, 0.0, p), jnp.where(lower, p, 0.0)],
                                      axis=0).astype(BF16))
        rows = slice(g * HEAD_DIM, (g + 1) * HEAD_DIM)
        vt2 = jnp.concatenate([vtp[rows, :], vtc[rows, :]], axis=1)
        o_t = jnp.dot(vt2, jnp.concatenate(ps, axis=1), preferred_element_type=F32)
        for r in range(0, per_group, 2):
            pair = jnp.concatenate([o_t[:, r * BLOCK:(r + 1) * BLOCK],
                                    o_t[:, (r + 1) * BLOCK:(r + 2) * BLOCK]], axis=0)
            outs.append(pair.T)
    return jnp.concatenate(outs, axis=1).astype(BF16)


def _swa_kernel(sink_ref, qt_ref, kp_ref, kc_ref, vtp_ref, vtc_ref, o_ref):
    m = pl.program_id(1)
    for b in range(SWA_BLOCKS):
        tok = slice(b * BLOCK, (b + 1) * BLOCK)
        kp = kp_ref[0] if b == 0 else kc_ref[0, (b - 1) * BLOCK:b * BLOCK, :]
        vtp = vtp_ref[0, 0] if b == 0 else vtc_ref[0, b - 1]
        o_ref[0, tok, :] = _swa_block(sink_ref, (m == 0) if b == 0 else False, qt_ref[0, b],
                                      kp, kc_ref[0, tok, :], vtp, vtc_ref[0, b])


def _swa(sink, aqt, ka, avt):
    bsz, nb = aqt.shape[:2]
    t = nb * BLOCK
    step = SWA_BLOCKS * BLOCK
    prev = lambda b, m: (b, jnp.maximum(m * SWA_BLOCKS - 1, 0), 0)
    prev4 = lambda b, m: (b, jnp.maximum(m * SWA_BLOCKS - 1, 0), 0, 0)
    return pl.pallas_call(
        _swa_kernel,
        grid=(bsz, nb // SWA_BLOCKS),
        in_specs=[
            pl.BlockSpec(memory_space=pltpu.SMEM),
            pl.BlockSpec((1, SWA_BLOCKS, LANES, A_HEADS * BLOCK), lambda b, m: (b, m, 0, 0)),
            pl.BlockSpec((1, BLOCK, A_KV * LANES), prev),
            pl.BlockSpec((1, step, A_KV * LANES), lambda b, m: (b, m, 0)),
            pl.BlockSpec((1, 1, LANES, BLOCK), prev4),
            pl.BlockSpec((1, SWA_BLOCKS, LANES, BLOCK), lambda b, m: (b, m, 0, 0)),
        ],
        out_specs=pl.BlockSpec((1, step, A_HEADS * HEAD_DIM), lambda b, m: (b, m, 0)),
        out_shape=jax.ShapeDtypeStruct((bsz, t, A_HEADS * HEAD_DIM), BF16),
        compiler_params=_cparams("parallel", "arbitrary"),
        name="swa",
    )(sink, aqt, ka, ka, avt, avt)


CK = 256
SUB = 128
POS_LANE = HEAD_DIM
PIPE = 4
ACC_ROWS = 80
TINY = 1.1754944e-38
FREE_STEPS = 14
MIN_DENOMINATOR = 2.0 ** -100


def _tree_sum(xs):
    while len(xs) > 1:
        xs = [a + b for a, b in zip(xs[0::2], xs[1::2])]
    return xs[0]


def _for_chunks(nck, unroll, body, carry):
    def group(i, carry):
        for u in range(unroll):
            carry = body(i * unroll + u, carry)
        return carry
    nfull = nck // unroll
    carry = lax.fori_loop(0, nfull, group, carry)
    return lax.fori_loop(nfull * unroll, nck, body, carry)


def _dsa_kernel(iqt_ref, iw_ref, bqt_ref, ki_ref, kb_ref, vt_ref, kn_ref, yb_ref, sc_ref, thr_ref, s_ref, p_ref,
                acc_ref):
    n = pl.program_id(1)
    nck = n // (CK // BLOCK) + 1
    tq = n * BLOCK + lax.broadcasted_iota(I32, (CK, LANES), 1)
    rel = lax.broadcasted_iota(I32, (CK, LANES), 0)

    qi_t = iqt_ref[0, 0]
    iw_t = (iw_ref[0] * (IDX_HEADS ** -0.5)).T

    def score_chunk(c, carry):
        smin, smax = carry
        start = pl.multiple_of(c * CK, CK)
        s = jnp.dot(ki_ref[0, pl.ds(start, CK), :], qi_t, preferred_element_type=F32)
        acc = _tree_sum([jnp.maximum(s[:, h * LANES:(h + 1) * LANES], 0.0) * iw_t[h:h + 1, :]
                         for h in range(IDX_HEADS)])
        causal = c * CK + rel <= tq
        sc_ref[c] = jnp.where(causal, acc, NEG)
        smax = jnp.maximum(smax, jnp.max(jnp.where(causal, acc, NEG), axis=0, keepdims=True))
        smin = jnp.minimum(smin, jnp.min(jnp.where(causal, acc, -NEG), axis=0, keepdims=True))
        return smin, smax

    smin, smax = _for_chunks(nck, 4, score_chunk,
                             (jnp.full((1, LANES), -NEG, F32), jnp.full((1, LANES), NEG, F32)))

    thr_ref[...] = jnp.full(thr_ref.shape, NEG, F32)

    @pl.when(n >= 2)
    def _():
        def count_ge(probe):
            def body(c, cnt):
                hit = jnp.where(sc_ref[c] >= probe, 1.0, 0.0)
                return cnt + jnp.sum(hit.reshape(CK // 32, 4, 8, LANES), axis=0)
            cnt = _for_chunks(nck, 4, body, jnp.zeros((4, 8, LANES), F32))
            return jnp.sum(cnt.reshape(32, LANES), axis=0, keepdims=True)

        def probe_of(lo, hi):
            p = 0.5 * lo + 0.5 * hi
            p = jnp.where((lo < 0.0) & (hi > 0.0), 0.0, p)
            p = jnp.where((lo == 0.0) & (hi > TINY), TINY, p)
            return jnp.where((hi == 0.0) & (lo < -TINY), -TINY, p)

        def cond(st):
            lo, hi, c_lo, c_hi = st
            p = probe_of(lo, hi)
            done = (c_lo == TOPK) | (hi - lo <= TINY) | ~((p > lo) & (p < hi))
            return jnp.max(jnp.where(done, 0.0, 1.0)) > 0.0

        def step(st):
            lo, hi, c_lo, c_hi = st
            p = probe_of(lo, hi)
            c = count_ge(p)
            ge = c >= TOPK
            return (jnp.where(ge, p, lo), jnp.where(ge, hi, p),
                    jnp.where(ge, c, c_lo), jnp.where(ge, c_hi, c))

        n_causal = (tq[0:1, :] + 1).astype(F32)
        st0 = (smin, smax + (jnp.abs(smax) * 1e-6 + TINY), n_causal, jnp.zeros((1, LANES), F32))
        st1 = lax.fori_loop(0, FREE_STEPS, lambda _, st: step(st), st0)
        lo, hi, c_lo, c_hi = lax.while_loop(cond, lambda st: step(step(st)), st1)
        thr_ref[...] = jnp.broadcast_to(lo, thr_ref.shape)

        tied = c_lo > TOPK

        @pl.when(jnp.max(jnp.where(tied, 1.0, 0.0)) > 0.0)
        def _():
            keep = TOPK - c_hi
            tri = jnp.where(lax.broadcasted_iota(I32, (CK, CK), 0) >= lax.broadcasted_iota(I32, (CK, CK), 1),
                            1.0, 0.0).astype(BF16)

            def drop(c, before):
                x = sc_ref[c]
                eq = x == lo
                prefix = before + jnp.dot(tri, jnp.where(eq, 1.0, 0.0).astype(BF16),
                                          preferred_element_type=F32)
                sc_ref[c] = jnp.where(eq & tied & (prefix > keep), NEG, x)
                return prefix[CK - 1:CK, :]

            lax.fori_loop(0, nck, drop, jnp.zeros((1, LANES), F32))

    thr = thr_ref[0:1, :]
    rel_s = lax.broadcasted_iota(I32, (SUB, LANES), 0)
    tq_s = n * BLOCK + lax.broadcasted_iota(I32, (SUB, LANES), 1)
    last = pl.num_programs(1) - 1
    q2_t = bqt_ref[0, 0]

    def key_rows(t):
        return kb_ref[0, pl.ds(pl.multiple_of(jnp.minimum(t, last) * SUB, SUB), SUB), :]

    def value_rows(t):
        return vt_ref[0, jnp.clip(t, 0, last), 0:ACC_ROWS, :]

    def select_bias(t):
        tc = jnp.minimum(t, n)
        rows = pl.ds(pl.multiple_of((tc % 2) * SUB, SUB), SUB)
        sel = (sc_ref[tc // 2, rows, :] >= thr) & (t * SUB + rel_s <= tq_s)
        return jnp.where(sel, 0.0, NEG)

    qf = q2_t[0:HEAD_DIM].astype(F32)
    q_norm2 = jnp.sum(qf * qf, axis=0, keepdims=True)
    k_norm2 = jnp.concatenate([kn_ref[0, 0:1, :]] * B_HEADS, axis=1)
    tq_row = jnp.concatenate([tq_s[0:1].astype(F32)] * B_HEADS, axis=1)
    slope_row = jnp.concatenate([jnp.full((1, LANES), float(np.float32(sum(t))), F32) for t in SLOPE2_TERMS],
                                axis=1)
    offset = jnp.sqrt(q_norm2 * k_norm2) * (1.0 + 2.0 ** -6) + 2.0 ** -6 + slope_row * tq_row
    terms, rest = [], -offset
    for _ in range(3):
        terms.append(rest.astype(BF16).astype(F32))
        rest = rest - terms[-1]
    row_q = lax.broadcasted_iota(I32, (LANES, B_HEADS * LANES), 0)
    q2_off = q2_t.astype(F32)
    for r, term in enumerate(terms):
        q2_off = jnp.where(row_q == POS_LANE + 6 + r, term, q2_off)
    q2_off = q2_off.astype(BF16)

    def stage_logits(t, k):
        s_ref[k] = jnp.dot(key_rows(t), q2_off, preferred_element_type=F32)

    def probabilities(t, k):
        negb = select_bias(t)
        for h in range(B_HEADS):
            cols = slice(h * LANES, (h + 1) * LANES)
            p_ref[k, :, cols] = jnp.exp2(s_ref[k, :, cols] + negb).astype(BF16)

    def pv_accumulate(t, k):
        values = jnp.concatenate([value_rows(t), value_rows(t + 1)], axis=1)
        probs = jnp.concatenate([p_ref[k], p_ref[k + 1]], axis=0)
        acc_ref[...] += jnp.dot(values, probs, preferred_element_type=F32)

    def group(g, carry):
        for k in range(0, PIPE, 2):
            pv_accumulate((g - 1) * PIPE + k, k)
            for kk in (k, k + 1):
                probabilities(g * PIPE + kk, kk)
                stage_logits((g + 1) * PIPE + kk, kk)
        return carry

    ngroups = n // PIPE + 1
    acc_ref[...] = jnp.zeros_like(acc_ref)
    for k in range(PIPE):
        stage_logits(k, k)
    p_ref[...] = jnp.zeros(p_ref.shape, BF16)
    lax.fori_loop(0, ngroups, group, 0)
    for k in range(0, PIPE, 2):
        pv_accumulate((ngroups - 1) * PIPE + k, k)

    @pl.when(jnp.logical_not(jnp.min(acc_ref[HEAD_DIM:HEAD_DIM + 1, :]) > MIN_DENOMINATOR))
    def _():
        acc_ref[...] = jnp.zeros_like(acc_ref)

        def exact_step(t, m):
            s = jnp.dot(key_rows(t), q2_t, preferred_element_type=F32)
            negb = select_bias(t)
            ps, alphas, m_out = [], [], []
            for h in range(B_HEADS):
                l = s[:, h * LANES:(h + 1) * LANES] + negb
                m_new = jnp.maximum(m[h], jnp.max(l, axis=0, keepdims=True))
                alphas.append(jnp.exp2(m[h] - m_new))
                ps.append(jnp.exp2(l - m_new).astype(BF16))
                m_out.append(m_new)
            pv = jnp.dot(value_rows(t), jnp.concatenate(ps, axis=1), preferred_element_type=F32)
            acc_ref[...] = acc_ref[...] * jnp.concatenate(alphas, axis=1) + pv
            return m_out

        lax.fori_loop(0, n + 1, exact_step, [jnp.full((1, LANES), NEG, F32) for _ in range(B_HEADS)])

    outs = []
    pad = jnp.zeros((LANES - HEAD_DIM, LANES), F32)
    for h in range(B_HEADS):
        blk = acc_ref[:, h * LANES:(h + 1) * LANES]
        o_t = blk[0:HEAD_DIM] / blk[HEAD_DIM:HEAD_DIM + 1]
        outs.append(jnp.concatenate([o_t, pad], axis=0).T[:, 0:HEAD_DIM])
    yb_ref[0] = jnp.concatenate(outs, axis=1).astype(BF16)


def _knorm_kernel(kb_ref, o_ref):
    k = kb_ref[0].astype(F32)
    lane = lax.broadcasted_iota(I32, k.shape, 1)
    sq = jnp.where(lane < HEAD_DIM, k * k, 0.0)
    o_ref[0] = jnp.full(o_ref.shape[1:], jnp.max(jnp.sum(sq, axis=1, keepdims=True)), F32)


def _knorm(kb):
    bsz, t, _ = kb.shape
    return pl.pallas_call(
        _knorm_kernel,
        grid=(bsz,),
        in_specs=[pl.BlockSpec((1, t, LANES), lambda b: (b, 0, 0))],
        out_specs=pl.BlockSpec((1, 8, LANES), lambda b: (b, 0, 0)),
        out_shape=jax.ShapeDtypeStruct((bsz, 8, LANES), F32),
        compiler_params=_cparams("parallel"),
        name="knorm",
    )(kb)


def _dsa(iqt, iw, bqt, ki, kb, vt, kn):
    bsz, t, _ = iw.shape
    nb = t // BLOCK
    return pl.pallas_call(
        _dsa_kernel,
        grid=(bsz, nb),
        in_specs=[
            pl.BlockSpec((1, 1, LANES, IDX_HEADS * BLOCK), lambda b, n: (b, n, 0, 0)),
            pl.BlockSpec((1, BLOCK, LANES), lambda b, n: (b, n, 0)),
            pl.BlockSpec((1, 1, LANES, B_HEADS * BLOCK), lambda b, n: (b, n, 0, 0)),
            pl.BlockSpec((1, t, LANES), lambda b, n: (b, 0, 0)),
            pl.BlockSpec((1, t, LANES), lambda b, n: (b, 0, 0)),
            pl.BlockSpec((1, nb, LANES, BLOCK), lambda b, n: (b, 0, 0, 0)),
            pl.BlockSpec((1, 8, LANES), lambda b, n: (b, 0, 0)),
        ],
        out_specs=pl.BlockSpec((1, BLOCK, B_HEADS * HEAD_DIM), lambda b, n: (b, n, 0)),
        out_shape=jax.ShapeDtypeStruct((bsz, t, B_HEADS * HEAD_DIM), BF16),
        scratch_shapes=[
            pltpu.VMEM((t // CK, CK, LANES), F32),
            pltpu.VMEM((8, LANES), F32),
            pltpu.VMEM((PIPE, SUB, B_HEADS * LANES), F32),
            pltpu.VMEM((PIPE, SUB, B_HEADS * LANES), BF16),
            pltpu.VMEM((ACC_ROWS, B_HEADS * LANES), F32),
        ],
        compiler_params=_cparams("parallel", "arbitrary"),
        name="dsa",
    )(iqt, iw, bqt, ki, kb, vt, kn)


MERGE_TM = 512


def _merge_kernel(h_ref, ya_ref, yb_ref, g_ref, wa_ref, wb_ref, wo_ref, o_ref):
    ta = jnp.dot(ya_ref[...], wa_ref[...], preferred_element_type=F32)
    tb = jnp.dot(yb_ref[...], wb_ref[...], preferred_element_type=F32)
    g = g_ref[...]
    mix = (g[:, :D_MODEL] * ta + g[:, D_MODEL:] * tb).astype(BF16)
    o_ref[...] = h_ref[...] + jnp.dot(mix, wo_ref[...], preferred_element_type=F32)


def _merge(h, ya, yb, g, wa, wb, wo):
    n = h.shape[0]
    row = lambda w: pl.BlockSpec((MERGE_TM, w), lambda i: (i, 0))
    full = lambda a: pl.BlockSpec(a.shape, lambda i: (0, 0))
    return pl.pallas_call(
        _merge_kernel,
        grid=(n // MERGE_TM,),
        in_specs=[row(D_MODEL), row(ya.shape[1]), row(yb.shape[1]), row(2 * D_MODEL),
                  full(wa), full(wb), full(wo)],
        out_specs=row(D_MODEL),
        out_shape=jax.ShapeDtypeStruct((n, D_MODEL), F32),
        compiler_params=_cparams("parallel"),
        name="merge",
    )(h, ya, yb, g, wa, wb, wo)


def _ple_kernel(h_ref, p_ref, lnp_ref, wg_ref, wp_ref, lnf_ref, o_ref):
    x = h_ref[...]
    u = _rms(x, lnp_ref[...]).astype(BF16)
    gate = jax.nn.sigmoid(jnp.dot(u, wg_ref[...], preferred_element_type=F32))
    proj = jnp.dot(p_ref[...].astype(BF16), wp_ref[...], preferred_element_type=F32)
    o_ref[...] = _rms(x + gate * proj, lnf_ref[...])


def _ple(h, p, lnp, wg, wp, lnf):
    n = h.shape[0]
    row = lambda w: pl.BlockSpec((MERGE_TM, w), lambda i: (i, 0))
    full = lambda a: pl.BlockSpec(a.shape, lambda i: (0, 0))
    return pl.pallas_call(
        _ple_kernel,
        grid=(n // MERGE_TM,),
        in_specs=[row(D_MODEL), row(D_PLE), full(lnp), full(wg), full(wp), full(lnf)],
        out_specs=row(D_MODEL),
        out_shape=jax.ShapeDtypeStruct((n, D_MODEL), F32),
        compiler_params=_cparams("parallel"),
        name="ple",
    )(h, p, lnp, wg, wp, lnf)


def kernel(x, p, ln_ffn1, w_ffn1_in, w_ffn1_out, ln_mix, w_in, a_sink, w_br_a, w_br_b, w_out,
           ln_ffn2, w_ffn2_in, w_ffn2_out, ln_ple, w_ple_gate, w_ple_proj, ln_final):
    bsz, t, d = x.shape
    assert p.shape[0] == 1, "the final norm is fused into the single layer's last kernel"
    n = bsz * t
    h = x.reshape(n, d)
    h = _ffn(h, ln_ffn1, w_ffn1_in[0].astype(BF16), w_ffn1_out[0].astype(BF16))
    aqt, ka, avt, iqt, bqt, ki, kb, vt, iw, g = _in_proj(h, ln_mix, _split_in_proj(w_in[0]), t // BLOCK)
    tok = lambda a: a.reshape(bsz, t, a.shape[-1])
    blk = lambda a: a.reshape(bsz, t // BLOCK, a.shape[-2], a.shape[-1])
    ya = _swa(a_sink[0], blk(aqt), tok(ka), blk(avt))
    yb = _dsa(blk(iqt), tok(iw), blk(bqt), tok(ki), tok(kb), blk(vt), _knorm(tok(kb)))
    h = _merge(h, ya.reshape(n, -1), yb.reshape(n, -1), g,
               w_br_a[0].astype(BF16), w_br_b[0].astype(BF16), w_out[0].astype(BF16))
    h = _ffn(h, ln_ffn2, w_ffn2_in[0].astype(BF16), w_ffn2_out[0].astype(BF16))
    h = _ple(h, p[0].reshape(n, -1), ln_ple, w_ple_gate[0].astype(BF16),
             w_ple_proj[0].astype(BF16), ln_final[None])
    return h.reshape(bsz, t, d)
```

```python
import functools

import numpy as np
import jax
import jax.numpy as jnp
from jax import lax
from jax.experimental import pallas as pl
from jax.experimental.pallas import tpu as pltpu

F32 = jnp.float32
BF16 = jnp.bfloat16
I32 = jnp.int32

D_MODEL = 1024
HEAD_DIM = 64
A_HEADS = 8
A_KV = 2
BLOCK = 128
B_HEADS = 8
IDX_HEADS = 8
TOPK = 256
D_FF = 2816
D_PLE = 256
EPS = 1e-6
NEG = -1e30

LANES = 128
VMEM_LIMIT = 48 * 1024 * 1024

_SLOPES = [float(np.float32(2.0) ** np.float32(-8.0 * i / (A_HEADS + B_HEADS)))
           for i in range(1, A_HEADS + B_HEADS + 1)]
SLOPES_A = _SLOPES[:A_HEADS]
SLOPES_B = _SLOPES[A_HEADS:]
LOG2E = float(np.log2(np.e))


def _bf16_terms(x, n=3):
    out = []
    for _ in range(n):
        t = float(np.float32(x).astype(jnp.bfloat16))
        out.append(t)
        x = x - t
    return out


SLOPE2_TERMS = [_bf16_terms(s * LOG2E) for s in SLOPES_B]


def _rms(x, g):
    return x * lax.rsqrt(jnp.mean(x * x, axis=-1, keepdims=True) + EPS) * g


def _cparams(*sem):
    return pltpu.CompilerParams(dimension_semantics=sem, vmem_limit_bytes=VMEM_LIMIT)


FFN_TM = 512


def _ffn_kernel(h_ref, ln_ref, wa_ref, wb_ref, wo_ref, o_ref):
    x = h_ref[...]
    xn = _rms(x, ln_ref[...]).astype(BF16)
    a = jnp.dot(xn, wa_ref[...], preferred_element_type=F32)
    b = jnp.dot(xn, wb_ref[...], preferred_element_type=F32)
    g = (a * jax.nn.sigmoid(a) * b).astype(BF16)
    o_ref[...] = x + 0.5 * jnp.dot(g, wo_ref[...], preferred_element_type=F32)


def _ffn(h, ln, w_in, w_out):
    n = h.shape[0]
    once = pl.Buffered(1)
    return pl.pallas_call(
        _ffn_kernel,
        grid=(n // FFN_TM,),
        in_specs=[
            pl.BlockSpec((FFN_TM, D_MODEL), lambda i: (i, 0)),
            pl.BlockSpec((1, D_MODEL), lambda i: (0, 0)),
            pl.BlockSpec((D_MODEL, D_FF), lambda i: (0, 0), pipeline_mode=once),
            pl.BlockSpec((D_MODEL, D_FF), lambda i: (0, 1), pipeline_mode=once),
            pl.BlockSpec((D_FF, D_MODEL), lambda i: (0, 0), pipeline_mode=once),
        ],
        out_specs=pl.BlockSpec((FFN_TM, D_MODEL), lambda i: (i, 0)),
        out_shape=jax.ShapeDtypeStruct((n, D_MODEL), F32),
        compiler_params=_cparams("parallel"),
        name="ffn",
    )(h, ln, w_in, w_in, w_out)


PROJ_TM = 256


def _inproj_kernel(blocks_per_seq, h_ref, ln_ref, waq, wka, wiq, wbq, wkk, wvv, wiw, wg,
                   o_aqt, o_ka, o_avt, o_iqt, o_bqt, o_ki, o_kb, o_vt, o_iw, o_g):
    u = _rms(h_ref[...], ln_ref[...]).astype(BF16)

    def mm(w):
        return jnp.dot(u, w[...], preferred_element_type=F32)

    o_ka[...] = mm(wka).astype(BF16)
    o_iw[...] = mm(wiw)
    o_g[...] = jax.nn.sigmoid(mm(wg))

    row = lax.broadcasted_iota(I32, (PROJ_TM, LANES), 0)
    lane = lax.broadcasted_iota(I32, (PROJ_TM, LANES), 1)
    block = ((pl.program_id(0) * PROJ_TM + row) // BLOCK) % blocks_per_seq
    pos = jnp.where(lane < POS_LANE + 3, (row % BLOCK).astype(F32),
                    jnp.where(lane < POS_LANE + 6, block.astype(F32), 1.0))
    pos = jnp.where((lane >= POS_LANE) & (lane < POS_LANE + 9), pos, 0.0)
    kk = mm(wkk)
    o_ki[...] = kk[:, 0:LANES].astype(BF16)
    o_kb[...] = (kk[:, LANES:2 * LANES] + pos).astype(BF16)
    vv = mm(wvv)
    v1 = jnp.where(lane == HEAD_DIM, 1.0, vv[:, 0:LANES])
    av = vv[:, LANES:2 * LANES]
    aq = mm(waq)
    iq = mm(wiq)
    bq = mm(wbq)
    row64 = lax.broadcasted_iota(I32, (HEAD_DIM, BLOCK), 0)
    zeros64 = jnp.zeros((HEAD_DIM, BLOCK), F32)
    for blk in range(PROJ_TM // BLOCK):
        tok = slice(blk * BLOCK, (blk + 1) * BLOCK)
        o_vt[blk] = v1[tok].T.astype(BF16)
        o_avt[blk] = av[tok].T.astype(BF16)
        for j in range(B_HEADS // 2):
            aq_t = aq[tok, j * LANES:(j + 1) * LANES].T
            iq_t = iq[tok, j * LANES:(j + 1) * LANES].T
            bq_t = bq[tok, j * LANES:(j + 1) * LANES].T
            for hh in range(2):
                h = 2 * j + hh
                cols = slice(h * BLOCK, (h + 1) * BLOCK)
                part = slice(hh * HEAD_DIM, (hh + 1) * HEAD_DIM)
                o_aqt[blk, :, cols] = jnp.concatenate([aq_t[part], zeros64], axis=0).astype(BF16)
                o_iqt[blk, :, cols] = jnp.concatenate([iq_t[part], zeros64], axis=0).astype(BF16)
                slope_rows = zeros64
                for r, term in enumerate(SLOPE2_TERMS[h]):
                    slope_rows = jnp.where(row64 == r, term, slope_rows)
                    slope_rows = jnp.where(row64 == r + 3, term * BLOCK, slope_rows)
                o_bqt[blk, :, cols] = jnp.concatenate([bq_t[part], slope_rows], axis=0).astype(BF16)


def _split_in_proj(w):
    scale = HEAD_DIM ** -0.5
    aq, ak, av = w[:, 0:512] * scale, w[:, 512:640], w[:, 640:768]
    bq, bk, bv = w[:, 768:1280] * (scale * LOG2E), w[:, 1280:1344], w[:, 1344:1408]
    iq, ik, iw = w[:, 1408:1920] * scale, w[:, 1920:1984], w[:, 1984:1992]
    g = w[:, 1992:4040]
    z64 = jnp.zeros((w.shape[0], HEAD_DIM), w.dtype)
    return dict(
        waq=aq.astype(BF16),
        wka=jnp.concatenate([ak[:, :HEAD_DIM], z64, ak[:, HEAD_DIM:], z64], axis=1).astype(BF16),
        wiq=iq.astype(BF16),
        wbq=bq.astype(BF16),
        wkk=jnp.concatenate([ik, z64, bk, z64], axis=1).astype(BF16),
        wvv=jnp.concatenate([bv, z64, av], axis=1).astype(BF16),
        wiw=jnp.pad(iw, ((0, 0), (0, LANES - IDX_HEADS))).astype(BF16),
        wg=g.astype(BF16),
    )


def _in_proj(h, ln, ws, blocks_per_seq):
    n = h.shape[0]
    blocks = PROJ_TM // BLOCK
    names = ["waq", "wka", "wiq", "wbq", "wkk", "wvv", "wiw", "wg"]
    rows = lambda wd, dt: (pl.BlockSpec((PROJ_TM, wd), lambda i: (i, 0)), jax.ShapeDtypeStruct((n, wd), dt))
    tiles = lambda r, c: (pl.BlockSpec((blocks, r, c), lambda i: (i, 0, 0)),
                          jax.ShapeDtypeStruct((n // BLOCK, r, c), BF16))
    outs = [tiles(LANES, A_HEADS * BLOCK), rows(A_KV * LANES, BF16), tiles(LANES, BLOCK),
            tiles(LANES, IDX_HEADS * BLOCK), tiles(LANES, B_HEADS * BLOCK),
            rows(LANES, BF16), rows(LANES, BF16), tiles(LANES, BLOCK),
            rows(LANES, F32), rows(2 * D_MODEL, F32)]
    return pl.pallas_call(
        functools.partial(_inproj_kernel, blocks_per_seq),
        grid=(n // PROJ_TM,),
        in_specs=[pl.BlockSpec((PROJ_TM, D_MODEL), lambda i: (i, 0)),
                  pl.BlockSpec((1, D_MODEL), lambda i: (0, 0))]
        + [pl.BlockSpec((D_MODEL, ws[k].shape[1]), lambda i: (0, 0)) for k in names],
        out_specs=[o[0] for o in outs],
        out_shape=[o[1] for o in outs],
        compiler_params=_cparams("parallel"),
        name="in_proj",
    )(h, ln, *[ws[k] for k in names])


SWA_BLOCKS = 2


def _swa_block(sink_ref, first, qt, kp, kc, vtp, vtc):
    j = lax.broadcasted_iota(I32, (BLOCK, BLOCK), 0)
    i = lax.broadcasted_iota(I32, (BLOCK, BLOCK), 1)
    lower = j <= i
    dcur = (i - j).astype(F32)
    dprev = (i - j + BLOCK).astype(F32)
    prev_bias = jnp.where(first, NEG, 0.0).astype(F32)
    per_group = A_HEADS // A_KV
    outs = []
    for g in range(A_KV):
        k2 = jnp.concatenate([kp[:, g * LANES:(g + 1) * LANES], kc[:, g * LANES:(g + 1) * LANES]], axis=0)
        s2 = jnp.dot(k2, qt[:, g * per_group * BLOCK:(g + 1) * per_group * BLOCK],
                     preferred_element_type=F32)
        ps = []
        for r in range(per_group):
            h = g * per_group + r
            sp = s2[0:BLOCK, r * BLOCK:(r + 1) * BLOCK]
            sc = s2[BLOCK:2 * BLOCK, r * BLOCK:(r + 1) * BLOCK]
            s = jnp.where(lower, sc - SLOPES_A[h] * dcur, sp - SLOPES_A[h] * dprev + prev_bias)
            sink = sink_ref[h]
            m = jnp.maximum(jnp.max(s, axis=0, keepdims=True), sink)
            e = jnp.exp(s - m)
            p = e / (jnp.sum(e, axis=0, keepdims=True) + jnp.exp(sink - m))
            ps.append(jnp.concatenate([jnp.where(lower, 0.0, p), jnp.where(lower, p, 0.0)],
                                      axis=0).astype(BF16))
        rows = slice(g * HEAD_DIM, (g + 1) * HEAD_DIM)
        vt2 = jnp.concatenate([vtp[rows, :], vtc[rows, :]], axis=1)
        o_t = jnp.dot(vt2, jnp.concatenate(ps, axis=1), preferred_element_type=F32)
        for r in range(0, per_group, 2):
            pair = jnp.concatenate([o_t[:, r * BLOCK:(r + 1) * BLOCK],
                                    o_t[:, (r + 1) * BLOCK:(r + 2) * BLOCK]], axis=0)
            outs.append(pair.T)
    return jnp.concatenate(outs, axis=1).astype(BF16)


def _swa_kernel(sink_ref, qt_ref, kp_ref, kc_ref, vtp_ref, vtc_ref, o_ref):
    m = pl.program_id(1)
    for b in range(SWA_BLOCKS):
        tok = slice(b * BLOCK, (b + 1) * BLOCK)
        kp = kp_ref[0] if b == 0 else kc_ref[0, (b - 1) * BLOCK:b * BLOCK, :]
        vtp = vtp_ref[0, 0] if b == 0 else vtc_ref[0, b - 1]
        o_ref[0, tok, :] = _swa_block(sink_ref, (m == 0) if b == 0 else False, qt_ref[0, b],
                                      kp, kc_ref[0, tok, :], vtp, vtc_ref[0, b])


def _swa(sink, aqt, ka, avt):
    bsz, nb = aqt.shape[:2]
    t = nb * BLOCK
    step = SWA_BLOCKS * BLOCK
    prev = lambda b, m: (b, jnp.maximum(m * SWA_BLOCKS - 1, 0), 0)
    prev4 = lambda b, m: (b, jnp.maximum(m * SWA_BLOCKS - 1, 0), 0, 0)
    return pl.pallas_call(
        _swa_kernel,
        grid=(bsz, nb // SWA_BLOCKS),
        in_specs=[
            pl.BlockSpec(memory_space=pltpu.SMEM),
            pl.BlockSpec((1, SWA_BLOCKS, LANES, A_HEADS * BLOCK), lambda b, m: (b, m, 0, 0)),
            pl.BlockSpec((1, BLOCK, A_KV * LANES), prev),
            pl.BlockSpec((1, step, A_KV * LANES), lambda b, m: (b, m, 0)),
            pl.BlockSpec((1, 1, LANES, BLOCK), prev4),
            pl.BlockSpec((1, SWA_BLOCKS, LANES, BLOCK), lambda b, m: (b, m, 0, 0)),
        ],
        out_specs=pl.BlockSpec((1, step, A_HEADS * HEAD_DIM), lambda b, m: (b, m, 0)),
        out_shape=jax.ShapeDtypeStruct((bsz, t, A_HEADS * HEAD_DIM), BF16),
        compiler_params=_cparams("parallel", "arbitrary"),
        name="swa",
    )(sink, aqt, ka, ka, avt, avt)


CK = 256
SUB = 128
POS_LANE = HEAD_DIM
PIPE = 4
ACC_ROWS = 80
TINY = 1.1754944e-38
FREE_STEPS = 14
MIN_DENOMINATOR = 2.0 ** -100


def _tree_sum(xs):
    while len(xs) > 1:
        xs = [a + b for a, b in zip(xs[0::2], xs[1::2])]
    return xs[0]


def _for_chunks(nck, unroll, body, carry):
    def group(i, carry):
        for u in range(unroll):
            carry = body(i * unroll + u, carry)
        return carry
    nfull = nck // unroll
    carry = lax.fori_loop(0, nfull, group, carry)
    return lax.fori_loop(nfull * unroll, nck, body, carry)


def _dsa_kernel(iqt_ref, iw_ref, bqt_ref, ki_ref, kb_ref, vt_ref, kn_ref, yb_ref, sc_ref, thr_ref, s_ref, p_ref,
                acc_ref):
    n = pl.program_id(1)
    nck = n // (CK // BLOCK) + 1
    tq = n * BLOCK + lax.broadcasted_iota(I32, (CK, LANES), 1)
    rel = lax.broadcasted_iota(I32, (CK, LANES), 0)

    qi_t = iqt_ref[0, 0]
    iw_t = (iw_ref[0] * (IDX_HEADS ** -0.5)).T

    def score_chunk(c, carry):
        smin, smax = carry
        start = pl.multiple_of(c * CK, CK)
        s = jnp.dot(ki_ref[0, pl.ds(start, CK), :], qi_t, preferred_element_type=F32)
        acc = _tree_sum([jnp.maximum(s[:, h * LANES:(h + 1) * LANES], 0.0) * iw_t[h:h + 1, :]
                         for h in range(IDX_HEADS)])
        causal = c * CK + rel <= tq
        sc_ref[c] = jnp.where(causal, acc, NEG)
        smax = jnp.maximum(smax, jnp.max(jnp.where(causal, acc, NEG), axis=0, keepdims=True))
        smin = jnp.minimum(smin, jnp.min(jnp.where(causal, acc, -NEG), axis=0, keepdims=True))
        return smin, smax

    smin, smax = _for_chunks(nck, 4, score_chunk,
                             (jnp.full((1, LANES), -NEG, F32), jnp.full((1, LANES), NEG, F32)))

    thr_ref[...] = jnp.full(thr_ref.shape, NEG, F32)

    @pl.when(n >= 2)
    def _():
        def count_ge(probe):
            def body(c, cnt):
                hit = jnp.where(sc_ref[c] >= probe, 1.0, 0.0)
                return cnt + jnp.sum(hit.reshape(CK // 32, 4, 8, LANES), axis=0)
            cnt = _for_chunks(nck, 4, body, jnp.zeros((4, 8, LANES), F32))
            return jnp.sum(cnt.reshape(32, LANES), axis=0, keepdims=True)

        def probe_of(lo, hi):
            p = 0.5 * lo + 0.5 * hi
            p = jnp.where((lo < 0.0) & (hi > 0.0), 0.0, p)
            p = jnp.where((lo == 0.0) & (hi > TINY), TINY, p)
            return jnp.where((hi == 0.0) & (lo < -TINY), -TINY, p)

        def cond(st):
            lo, hi, c_lo, c_hi = st
            p = probe_of(lo, hi)
            done = (c_lo == TOPK) | (hi - lo <= TINY) | ~((p > lo) & (p < hi))
            return jnp.max(jnp.where(done, 0.0, 1.0)) > 0.0

        def step(st):
            lo, hi, c_lo, c_hi = st
            p = probe_of(lo, hi)
            c = count_ge(p)
            ge = c >= TOPK
            return (jnp.where(ge, p, lo), jnp.where(ge, hi, p),
                    jnp.where(ge, c, c_lo), jnp.where(ge, c_hi, c))

        n_causal = (tq[0:1, :] + 1).astype(F32)
        st0 = (smin, smax + (jnp.abs(smax) * 1e-6 + TINY), n_causal, jnp.zeros((1, LANES), F32))
        st1 = lax.fori_loop(0, FREE_STEPS, lambda _, st: step(st), st0)
        lo, hi, c_lo, c_hi = lax.while_loop(cond, lambda st: step(step(st)), st1)
        thr_ref[...] = jnp.broadcast_to(lo, thr_ref.shape)

        tied = c_lo > TOPK

        @pl.when(jnp.max(jnp.where(tied, 1.0, 0.0)) > 0.0)
        def _():
            keep = TOPK - c_hi
            tri = jnp.where(lax.broadcasted_iota(I32, (CK, CK), 0) >= lax.broadcasted_iota(I32, (CK, CK), 1),
                            1.0, 0.0).astype(BF16)

            def drop(c, before):
                x = sc_ref[c]
                eq = x == lo
                prefix = before + jnp.dot(tri, jnp.where(eq, 1.0, 0.0).astype(BF16),
                                          preferred_element_type=F32)
                sc_ref[c] = jnp.where(eq & tied & (prefix > keep), NEG, x)
                return prefix[CK - 1:CK, :]

            lax.fori_loop(0, nck, drop, jnp.zeros((1, LANES), F32))

    thr = thr_ref[0:1, :]
    rel_s = lax.broadcasted_iota(I32, (SUB, LANES), 0)
    tq_s = n * BLOCK + lax.broadcasted_iota(I32, (SUB, LANES), 1)
    last = pl.num_programs(1) - 1
    q2_t = bqt_ref[0, 0]

    def key_rows(t):
        return kb_ref[0, pl.ds(pl.multiple_of(jnp.minimum(t, last) * SUB, SUB), SUB), :]

    def value_rows(t):
        return vt_ref[0, jnp.clip(t, 0, last), 0:ACC_ROWS, :]

    def select_bias(t):
        tc = jnp.minimum(t, n)
        rows = pl.ds(pl.multiple_of((tc % 2) * SUB, SUB), SUB)
        sel = (sc_ref[tc // 2, rows, :] >= thr) & (t * SUB + rel_s <= tq_s)
        return jnp.where(sel, 0.0, NEG)

    qf = q2_t[0:HEAD_DIM].astype(F32)
    q_norm2 = jnp.sum(qf * qf, axis=0, keepdims=True)
    k_norm2 = jnp.concatenate([kn_ref[0, 0:1, :]] * B_HEADS, axis=1)
    tq_row = jnp.concatenate([tq_s[0:1].astype(F32)] * B_HEADS, axis=1)
    slope_row = jnp.concatenate([jnp.full((1, LANES), float(np.float32(sum(t))), F32) for t in SLOPE2_TERMS],
                                axis=1)
    offset = jnp.sqrt(q_norm2 * k_norm2) * (1.0 + 2.0 ** -6) + 2.0 ** -6 + slope_row * tq_row
    terms, rest = [], -offset
    for _ in range(3):
        terms.append(rest.astype(BF16).astype(F32))
        rest = rest - terms[-1]
    row_q = lax.broadcasted_iota(I32, (LANES, B_HEADS * LANES), 0)
    q2_off = q2_t.astype(F32)
    for r, term in enumerate(terms):
        q2_off = jnp.where(row_q == POS_LANE + 6 + r, term, q2_off)
    q2_off = q2_off.astype(BF16)

    def stage_logits(t, k):
        s_ref[k] = jnp.dot(key_rows(t), q2_off, preferred_element_type=F32)

    def probabilities(t, k):
        negb = select_bias(t)
        for h in range(B_HEADS):
            cols = slice(h * LANES, (h + 1) * LANES)
            p_ref[k, :, cols] = jnp.exp2(s_ref[k, :, cols] + negb).astype(BF16)

    def pv_accumulate(t, k):
        values = jnp.concatenate([value_rows(t), value_rows(t + 1)], axis=1)
        probs = jnp.concatenate([p_ref[k], p_ref[k + 1]], axis=0)
        acc_ref[...] += jnp.dot(values, probs, preferred_element_type=F32)

    def group(g, carry):
        for k in range(0, PIPE, 2):
            pv_accumulate((g - 1) * PIPE + k, k)
            for kk in (k, k + 1):
                probabilities(g * PIPE + kk, kk)
                stage_logits((g + 1) * PIPE + kk, kk)
        return carry

    ngroups = n // PIPE + 1
    acc_ref[...] = jnp.zeros_like(acc_ref)
    for k in range(PIPE):
        stage_logits(k, k)
    p_ref[...] = jnp.zeros(p_ref.shape, BF16)
    lax.fori_loop(0, ngroups, group, 0)
    for k in range(0, PIPE, 2):
        pv_accumulate((ngroups - 1) * PIPE + k, k)

    @pl.when(jnp.logical_not(jnp.min(acc_ref[HEAD_DIM:HEAD_DIM + 1, :]) > MIN_DENOMINATOR))
    def _():
        acc_ref[...] = jnp.zeros_like(acc_ref)

        def exact_step(t, m):
            s = jnp.dot(key_rows(t), q2_t, preferred_element_type=F32)
            negb = select_bias(t)
            ps, alphas, m_out = [], [], []
            for h in range(B_HEADS):
                l = s[:, h * LANES:(h + 1) * LANES] + negb
                m_new = jnp.maximum(m[h], jnp.max(l, axis=0, keepdims=True))
                alphas.append(jnp.exp2(m[h] - m_new))
                ps.append(jnp.exp2(l - m_new).astype(BF16))
                m_out.append(m_new)
            pv = jnp.dot(value_rows(t), jnp.concatenate(ps, axis=1), preferred_element_type=F32)
            acc_ref[...] = acc_ref[...] * jnp.concatenate(alphas, axis=1) + pv
            return m_out

        lax.fori_loop(0, n + 1, exact_step, [jnp.full((1, LANES), NEG, F32) for _ in range(B_HEADS)])

    outs = []
    pad = jnp.zeros((LANES - HEAD_DIM, LANES), F32)
    for h in range(B_HEADS):
        blk = acc_ref[:, h * LANES:(h + 1) * LANES]
        o_t = blk[0:HEAD_DIM] / blk[HEAD_DIM:HEAD_DIM + 1]
        outs.append(jnp.concatenate([o_t, pad], axis=0).T[:, 0:HEAD_DIM])
    yb_ref[0] = jnp.concatenate(outs, axis=1).astype(BF16)


def _knorm_kernel(kb_ref, o_ref):
    k = kb_ref[0].astype(F32)
    lane = lax.broadcasted_iota(I32, k.shape, 1)
    sq = jnp.where(lane < HEAD_DIM, k * k, 0.0)
    o_ref[0] = jnp.full(o_ref.shape[1:], jnp.max(jnp.sum(sq, axis=1, keepdims=True)), F32)


def _knorm(kb):
    bsz, t, _ = kb.shape
    return pl.pallas_call(
        _knorm_kernel,
        grid=(bsz,),
        in_specs=[pl.BlockSpec((1, t, LANES), lambda b: (b, 0, 0))],
        out_specs=pl.BlockSpec((1, 8, LANES), lambda b: (b, 0, 0)),
        out_shape=jax.ShapeDtypeStruct((bsz, 8, LANES), F32),
        compiler_params=_cparams("parallel"),
        name="knorm",
    )(kb)


def _dsa(iqt, iw, bqt, ki, kb, vt, kn):
    bsz, t, _ = iw.shape
    nb = t // BLOCK
    return pl.pallas_call(
        _dsa_kernel,
        grid=(bsz, nb),
        in_specs=[
            pl.BlockSpec((1, 1, LANES, IDX_HEADS * BLOCK), lambda b, n: (b, n, 0, 0)),
            pl.BlockSpec((1, BLOCK, LANES), lambda b, n: (b, n, 0)),
            pl.BlockSpec((1, 1, LANES, B_HEADS * BLOCK), lambda b, n: (b, n, 0, 0)),
            pl.BlockSpec((1, t, LANES), lambda b, n: (b, 0, 0)),
            pl.BlockSpec((1, t, LANES), lambda b, n: (b, 0, 0)),
            pl.BlockSpec((1, nb, LANES, BLOCK), lambda b, n: (b, 0, 0, 0)),
            pl.BlockSpec((1, 8, LANES), lambda b, n: (b, 0, 0)),
        ],
        out_specs=pl.BlockSpec((1, BLOCK, B_HEADS * HEAD_DIM), lambda b, n: (b, n, 0)),
        out_shape=jax.ShapeDtypeStruct((bsz, t, B_HEADS * HEAD_DIM), BF16),
        scratch_shapes=[
            pltpu.VMEM((t // CK, CK, LANES), F32),
            pltpu.VMEM((8, LANES), F32),
            pltpu.VMEM((PIPE, SUB, B_HEADS * LANES), F32),
            pltpu.VMEM((PIPE, SUB, B_HEADS * LANES), BF16),
            pltpu.VMEM((ACC_ROWS, B_HEADS * LANES), F32),
        ],
        compiler_params=_cparams("parallel", "arbitrary"),
        name="dsa",
    )(iqt, iw, bqt, ki, kb, vt, kn)


MERGE_TM = 512


def _merge_kernel(h_ref, ya_ref, yb_ref, g_ref, wa_ref, wb_ref, wo_ref, o_ref):
    ta = jnp.dot(ya_ref[...], wa_ref[...], preferred_element_type=F32)
    tb = jnp.dot(yb_ref[...], wb_ref[...], preferred_element_type=F32)
    g = g_ref[...]
    mix = (g[:, :D_MODEL] * ta + g[:, D_MODEL:] * tb).astype(BF16)
    o_ref[...] = h_ref[...] + jnp.dot(mix, wo_ref[...], preferred_element_type=F32)


def _merge(h, ya, yb, g, wa, wb, wo):
    n = h.shape[0]
    row = lambda w: pl.BlockSpec((MERGE_TM, w), lambda i: (i, 0))
    full = lambda a: pl.BlockSpec(a.shape, lambda i: (0, 0))
    return pl.pallas_call(
        _merge_kernel,
        grid=(n // MERGE_TM,),
        in_specs=[row(D_MODEL), row(ya.shape[1]), row(yb.shape[1]), row(2 * D_MODEL),
                  full(wa), full(wb), full(wo)],
        out_specs=row(D_MODEL),
        out_shape=jax.ShapeDtypeStruct((n, D_MODEL), F32),
        compiler_params=_cparams("parallel"),
        name="merge",
    )(h, ya, yb, g, wa, wb, wo)


def _ple_kernel(h_ref, p_ref, lnp_ref, wg_ref, wp_ref, lnf_ref, o_ref):
    x = h_ref[...]
    u = _rms(x, lnp_ref[...]).astype(BF16)
    gate = jax.nn.sigmoid(jnp.dot(u, wg_ref[...], preferred_element_type=F32))
    proj = jnp.dot(p_ref[...].astype(BF16), wp_ref[...], preferred_element_type=F32)
    o_ref[...] = _rms(x + gate * proj, lnf_ref[...])


def _ple(h, p, lnp, wg, wp, lnf):
    n = h.shape[0]
    row = lambda w: pl.BlockSpec((MERGE_TM, w), lambda i: (i, 0))
    full = lambda a: pl.BlockSpec(a.shape, lambda i: (0, 0))
    return pl.pallas_call(
        _ple_kernel,
        grid=(n // MERGE_TM,),
        in_specs=[row(D_MODEL), row(D_PLE), full(lnp), full(wg), full(wp), full(lnf)],
        out_specs=row(D_MODEL),
        out_shape=jax.ShapeDtypeStruct((n, D_MODEL), F32),
        compiler_params=_cparams("parallel"),
        name="ple",
    )(h, p, lnp, wg, wp, lnf)


def kernel(x, p, ln_ffn1, w_ffn1_in, w_ffn1_out, ln_mix, w_in, a_sink, w_br_a, w_br_b, w_out,
           ln_ffn2, w_ffn2_in, w_ffn2_out, ln_ple, w_ple_gate, w_ple_proj, ln_final):
    bsz, t, d = x.shape
    assert p.shape[0] == 1, "the final norm is fused into the single layer's last kernel"
    n = bsz * t
    h = x.reshape(n, d)
    h = _ffn(h, ln_ffn1, w_ffn1_in[0].astype(BF16), w_ffn1_out[0].astype(BF16))
    aqt, ka, avt, iqt, bqt, ki, kb, vt, iw, g = _in_proj(h, ln_mix, _split_in_proj(w_in[0]), t // BLOCK)
    tok = lambda a: a.reshape(bsz, t, a.shape[-1])
    blk = lambda a: a.reshape(bsz, t // BLOCK, a.shape[-2], a.shape[-1])
    ya = _swa(a_sink[0], blk(aqt), tok(ka), blk(avt))
    yb = _dsa(blk(iqt), tok(iw), blk(bqt), tok(ki), tok(kb), blk(vt), _knorm(tok(kb)))
    h = _merge(h, ya.reshape(n, -1), yb.reshape(n, -1), g,
               w_br_a[0].astype(BF16), w_br_b[0].astype(BF16), w_out[0].astype(BF16))
    h = _ffn(h, ln_ffn2, w_ffn2_in[0].astype(BF16), w_ffn2_out[0].astype(BF16))
    h = _ple(h, p[0].reshape(n, -1), ln_ple, w_ple_gate[0].astype(BF16),
             w_ple_proj[0].astype(BF16), ln_final[None])
    return h.reshape(bsz, t, d)
```

```python
import functools

import numpy as np
import jax
import jax.numpy as jnp
from jax import lax
from jax.experimental import pallas as pl
from jax.experimental.pallas import tpu as pltpu

F32 = jnp.float32
BF16 = jnp.bfloat16
I32 = jnp.int32

D_MODEL = 1024
HEAD_DIM = 64
A_HEADS = 8
A_KV = 2
BLOCK = 128
B_HEADS = 8
IDX_HEADS = 8
TOPK = 256
D_FF = 2816
D_PLE = 256
EPS = 1e-6
NEG = -1e30

LANES = 128
VMEM_LIMIT = 48 * 1024 * 1024

_SLOPES = [float(np.float32(2.0) ** np.float32(-8.0 * i / (A_HEADS + B_HEADS)))
           for i in range(1, A_HEADS + B_HEADS + 1)]
SLOPES_A = _SLOPES[:A_HEADS]
SLOPES_B = _SLOPES[A_HEADS:]
LOG2E = float(np.log2(np.e))


def _bf16_terms(x, n=3):
    out = []
    for _ in range(n):
        t = float(np.float32(x).astype(jnp.bfloat16))
        out.append(t)
        x = x - t
    return out


SLOPE2_TERMS = [_bf16_terms(s * LOG2E) for s in SLOPES_B]


def _rms(x, g):
    return x * lax.rsqrt(jnp.mean(x * x, axis=-1, keepdims=True) + EPS) * g


def _cparams(*sem):
    return pltpu.CompilerParams(dimension_semantics=sem, vmem_limit_bytes=VMEM_LIMIT)


FFN_TM = 512


def _ffn_kernel(h_ref, ln_ref, wa_ref, wb_ref, wo_ref, o_ref):
    x = h_ref[...]
    xn = _rms(x, ln_ref[...]).astype(BF16)
    a = jnp.dot(xn, wa_ref[...], preferred_element_type=F32)
    b = jnp.dot(xn, wb_ref[...], preferred_element_type=F32)
    g = (a * jax.nn.sigmoid(a) * b).astype(BF16)
    o_ref[...] = x + 0.5 * jnp.dot(g, wo_ref[...], preferred_element_type=F32)


def _ffn(h, ln, w_in, w_out):
    n = h.shape[0]
    once = pl.Buffered(1)
    return pl.pallas_call(
        _ffn_kernel,
        grid=(n // FFN_TM,),
        in_specs=[
            pl.BlockSpec((FFN_TM, D_MODEL), lambda i: (i, 0)),
            pl.BlockSpec((1, D_MODEL), lambda i: (0, 0)),
            pl.BlockSpec((D_MODEL, D_FF), lambda i: (0, 0), pipeline_mode=once),
            pl.BlockSpec((D_MODEL, D_FF), lambda i: (0, 1), pipeline_mode=once),
            pl.BlockSpec((D_FF, D_MODEL), lambda i: (0, 0), pipeline_mode=once),
        ],
        out_specs=pl.BlockSpec((FFN_TM, D_MODEL), lambda i: (i, 0)),
        out_shape=jax.ShapeDtypeStruct((n, D_MODEL), F32),
        compiler_params=_cparams("parallel"),
        name="ffn",
    )(h, ln, w_in, w_in, w_out)


PROJ_TM = 256


def _inproj_kernel(blocks_per_seq, h_ref, ln_ref, waq, wka, wiq, wbq, wkk, wvv, wiw, wg,
                   o_aqt, o_ka, o_avt, o_iqt, o_bqt, o_ki, o_kb, o_vt, o_iw, o_g):
    u = _rms(h_ref[...], ln_ref[...]).astype(BF16)

    def mm(w):
        return jnp.dot(u, w[...], preferred_element_type=F32)

    o_ka[...] = mm(wka).astype(BF16)
    o_iw[...] = mm(wiw)
    o_g[...] = jax.nn.sigmoid(mm(wg))

    row = lax.broadcasted_iota(I32, (PROJ_TM, LANES), 0)
    lane = lax.broadcasted_iota(I32, (PROJ_TM, LANES), 1)
    block = ((pl.program_id(0) * PROJ_TM + row) // BLOCK) % blocks_per_seq
    pos = jnp.where(lane < POS_LANE + 3, (row % BLOCK).astype(F32),
                    jnp.where(lane < POS_LANE + 6, block.astype(F32), 1.0))
    pos = jnp.where((lane >= POS_LANE) & (lane < POS_LANE + 9), pos, 0.0)
    kk = mm(wkk)
    o_ki[...] = kk[:, 0:LANES].astype(BF16)
    o_kb[...] = (kk[:, LANES:2 * LANES] + pos).astype(BF16)
    vv = mm(wvv)
    v1 = jnp.where(lane == HEAD_DIM, 1.0, vv[:, 0:LANES])
    av = vv[:, LANES:2 * LANES]
    aq = mm(waq)
    iq = mm(wiq)
    bq = mm(wbq)
    row64 = lax.broadcasted_iota(I32, (HEAD_DIM, BLOCK), 0)
    zeros64 = jnp.zeros((HEAD_DIM, BLOCK), F32)
    for blk in range(PROJ_TM // BLOCK):
        tok = slice(blk * BLOCK, (blk + 1) * BLOCK)
        o_vt[blk] = v1[tok].T.astype(BF16)
        o_avt[blk] = av[tok].T.astype(BF16)
        for j in range(B_HEADS // 2):
            aq_t = aq[tok, j * LANES:(j + 1) * LANES].T
            iq_t = iq[tok, j * LANES:(j + 1) * LANES].T
            bq_t = bq[tok, j * LANES:(j + 1) * LANES].T
            for hh in range(2):
                h = 2 * j + hh
                cols = slice(h * BLOCK, (h + 1) * BLOCK)
                part = slice(hh * HEAD_DIM, (hh + 1) * HEAD_DIM)
                o_aqt[blk, :, cols] = jnp.concatenate([aq_t[part], zeros64], axis=0).astype(BF16)
                o_iqt[blk, :, cols] = jnp.concatenate([iq_t[part], zeros64], axis=0).astype(BF16)
                slope_rows = zeros64
                for r, term in enumerate(SLOPE2_TERMS[h]):
                    slope_rows = jnp.where(row64 == r, term, slope_rows)
                    slope_rows = jnp.where(row64 == r + 3, term * BLOCK, slope_rows)
                o_bqt[blk, :, cols] = jnp.concatenate([bq_t[part], slope_rows], axis=0).astype(BF16)


def _split_in_proj(w):
    scale = HEAD_DIM ** -0.5
    aq, ak, av = w[:, 0:512] * scale, w[:, 512:640], w[:, 640:768]
    bq, bk, bv = w[:, 768:1280] * (scale * LOG2E), w[:, 1280:1344], w[:, 1344:1408]
    iq, ik, iw = w[:, 1408:1920] * scale, w[:, 1920:1984], w[:, 1984:1992]
    g = w[:, 1992:4040]
    z64 = jnp.zeros((w.shape[0], HEAD_DIM), w.dtype)
    return dict(
        waq=aq.astype(BF16),
        wka=jnp.concatenate([ak[:, :HEAD_DIM], z64, ak[:, HEAD_DIM:], z64], axis=1).astype(BF16),
        wiq=iq.astype(BF16),
        wbq=bq.astype(BF16),
        wkk=jnp.concatenate([ik, z64, bk, z64], axis=1).astype(BF16),
        wvv=jnp.concatenate([bv, z64, av], axis=1).astype(BF16),
        wiw=jnp.pad(iw, ((0, 0), (0, LANES - IDX_HEADS))).astype(BF16),
        wg=g.astype(BF16),
    )


def _in_proj(h, ln, ws, blocks_per_seq):
    n = h.shape[0]
    blocks = PROJ_TM // BLOCK
    names = ["waq", "wka", "wiq", "wbq", "wkk", "wvv", "wiw", "wg"]
    rows = lambda wd, dt: (pl.BlockSpec((PROJ_TM, wd), lambda i: (i, 0)), jax.ShapeDtypeStruct((n, wd), dt))
    tiles = lambda r, c: (pl.BlockSpec((blocks, r, c), lambda i: (i, 0, 0)),
                          jax.ShapeDtypeStruct((n // BLOCK, r, c), BF16))
    outs = [tiles(LANES, A_HEADS * BLOCK), rows(A_KV * LANES, BF16), tiles(LANES, BLOCK),
            tiles(LANES, IDX_HEADS * BLOCK), tiles(LANES, B_HEADS * BLOCK),
            rows(LANES, BF16), rows(LANES, BF16), tiles(LANES, BLOCK),
            rows(LANES, F32), rows(2 * D_MODEL, F32)]
    return pl.pallas_call(
        functools.partial(_inproj_kernel, blocks_per_seq),
        grid=(n // PROJ_TM,),
        in_specs=[pl.BlockSpec((PROJ_TM, D_MODEL), lambda i: (i, 0)),
                  pl.BlockSpec((1, D_MODEL), lambda i: (0, 0))]
        + [pl.BlockSpec((D_MODEL, ws[k].shape[1]), lambda i: (0, 0)) for k in names],
        out_specs=[o[0] for o in outs],
        out_shape=[o[1] for o in outs],
        compiler_params=_cparams("parallel"),
        name="in_proj",
    )(h, ln, *[ws[k] for k in names])


SWA_BLOCKS = 2


def _swa_block(sink_ref, first, qt, kp, kc, vtp, vtc):
    j = lax.broadcasted_iota(I32, (BLOCK, BLOCK), 0)
    i = lax.broadcasted_iota(I32, (BLOCK, BLOCK), 1)
    lower = j <= i
    dcur = (i - j).astype(F32)
    dprev = (i - j + BLOCK).astype(F32)
    prev_bias = jnp.where(first, NEG, 0.0).astype(F32)
    per_group = A_HEADS // A_KV
    outs = []
    for g in range(A_KV):
        k2 = jnp.concatenate([kp[:, g * LANES:(g + 1) * LANES], kc[:, g * LANES:(g + 1) * LANES]], axis=0)
        s2 = jnp.dot(k2, qt[:, g * per_group * BLOCK:(g + 1) * per_group * BLOCK],
                     preferred_element_type=F32)
        ps = []
        for r in range(per_group):
            h = g * per_group + r
            sp = s2[0:BLOCK, r * BLOCK:(r + 1) * BLOCK]
            sc = s2[BLOCK:2 * BLOCK, r * BLOCK:(r + 1) * BLOCK]
            s = jnp.where(lower, sc - SLOPES_A[h] * dcur, sp - SLOPES_A[h] * dprev + prev_bias)
            sink = sink_ref[h]
            m = jnp.maximum(jnp.max(s, axis=0, keepdims=True), sink)
            e = jnp.exp(s - m)
            p = e / (jnp.sum(e, axis=0, keepdims=True) + jnp.exp(sink - m))
            ps.append(jnp.concatenate([jnp.where(lower, 0.0, p), jnp.where(lower, p, 0.0)],
                                      axis=0).astype(BF16))
        rows = slice(g * HEAD_DIM, (g + 1) * HEAD_DIM)
        vt2 = jnp.concatenate([vtp[rows, :], vtc[rows, :]], axis=1)
        o_t = jnp.dot(vt2, jnp.concatenate(ps, axis=1), preferred_element_type=F32)
        for r in range(0, per_group, 2):
            pair = jnp.concatenate([o_t[:, r * BLOCK:(r + 1) * BLOCK],
                                    o_t[:, (r + 1) * BLOCK:(r + 2) * BLOCK]], axis=0)
            outs.append(pair.T)
    return jnp.concatenate(outs, axis=1).astype(BF16)


def _swa_kernel(sink_ref, qt_ref, kp_ref, kc_ref, vtp_ref, vtc_ref, o_ref):
    m = pl.program_id(1)
    for b in range(SWA_BLOCKS):
        tok = slice(b * BLOCK, (b + 1) * BLOCK)
        kp = kp_ref[0] if b == 0 else kc_ref[0, (b - 1) * BLOCK:b * BLOCK, :]
        vtp = vtp_ref[0, 0] if b == 0 else vtc_ref[0, b - 1]
        o_ref[0, tok, :] = _swa_block(sink_ref, (m == 0) if b == 0 else False, qt_ref[0, b],
                                      kp, kc_ref[0, tok, :], vtp, vtc_ref[0, b])


def _swa(sink, aqt, ka, avt):
    bsz, nb = aqt.shape[:2]
    t = nb * BLOCK
    step = SWA_BLOCKS * BLOCK
    prev = lambda b, m: (b, jnp.maximum(m * SWA_BLOCKS - 1, 0), 0)
    prev4 = lambda b, m: (b, jnp.maximum(m * SWA_BLOCKS - 1, 0), 0, 0)
    return pl.pallas_call(
        _swa_kernel,
        grid=(bsz, nb // SWA_BLOCKS),
        in_specs=[
            pl.BlockSpec(memory_space=pltpu.SMEM),
            pl.BlockSpec((1, SWA_BLOCKS, LANES, A_HEADS * BLOCK), lambda b, m: (b, m, 0, 0)),
            pl.BlockSpec((1, BLOCK, A_KV * LANES), prev),
            pl.BlockSpec((1, step, A_KV * LANES), lambda b, m: (b, m, 0)),
            pl.BlockSpec((1, 1, LANES, BLOCK), prev4),
            pl.BlockSpec((1, SWA_BLOCKS, LANES, BLOCK), lambda b, m: (b, m, 0, 0)),
        ],
        out_specs=pl.BlockSpec((1, step, A_HEADS * HEAD_DIM), lambda b, m: (b, m, 0)),
        out_shape=jax.ShapeDtypeStruct((bsz, t, A_HEADS * HEAD_DIM), BF16),
        compiler_params=_cparams("parallel", "arbitrary"),
        name="swa",
    )(sink, aqt, ka, ka, avt, avt)


CK = 256
SUB = 128
POS_LANE = HEAD_DIM
PIPE = 4
ACC_ROWS = 80
TINY = 1.1754944e-38
FREE_STEPS = 14
MIN_DENOMINATOR = 2.0 ** -100


def _tree_sum(xs):
    while len(xs) > 1:
        xs = [a + b for a, b in zip(xs[0::2], xs[1::2])]
    return xs[0]


def _for_chunks(nck, unroll, body, carry):
    def group(i, carry):
        for u in range(unroll):
            carry = body(i * unroll + u, carry)
        return carry
    nfull = nck // unroll
    carry = lax.fori_loop(0, nfull, group, carry)
    return lax.fori_loop(nfull * unroll, nck, body, carry)


def _dsa_kernel(iqt_ref, iw_ref, bqt_ref, ki_ref, kb_ref, vt_ref, kn_ref, yb_ref, sc_ref, thr_ref, s_ref, p_ref,
                acc_ref):
    n = pl.program_id(1)
    nck = n // (CK // BLOCK) + 1
    tq = n * BLOCK + lax.broadcasted_iota(I32, (CK, LANES), 1)
    rel = lax.broadcasted_iota(I32, (CK, LANES), 0)

    qi_t = iqt_ref[0, 0]
    iw_t = (iw_ref[0] * (IDX_HEADS ** -0.5)).T

    def score_chunk(c, carry):
        smin, smax = carry
        start = pl.multiple_of(c * CK, CK)
        s = jnp.dot(ki_ref[0, pl.ds(start, CK), :], qi_t, preferred_element_type=F32)
        acc = _tree_sum([jnp.maximum(s[:, h * LANES:(h + 1) * LANES], 0.0) * iw_t[h:h + 1, :]
                         for h in range(IDX_HEADS)])
        causal = c * CK + rel <= tq
        sc_ref[c] = jnp.where(causal, acc, NEG)
        smax = jnp.maximum(smax, jnp.max(jnp.where(causal, acc, NEG), axis=0, keepdims=True))
        smin = jnp.minimum(smin, jnp.min(jnp.where(causal, acc, -NEG), axis=0, keepdims=True))
        return smin, smax

    smin, smax = _for_chunks(nck, 4, score_chunk,
                             (jnp.full((1, LANES), -NEG, F32), jnp.full((1, LANES), NEG, F32)))

    thr_ref[...] = jnp.full(thr_ref.shape, NEG, F32)

    @pl.when(n >= 2)
    def _():
        def count_ge(probe):
            def body(c, cnt):
                hit = jnp.where(sc_ref[c] >= probe, 1.0, 0.0)
                return cnt + jnp.sum(hit.reshape(CK // 32, 4, 8, LANES), axis=0)
            cnt = _for_chunks(nck, 4, body, jnp.zeros((4, 8, LANES), F32))
            return jnp.sum(cnt.reshape(32, LANES), axis=0, keepdims=True)

        def probe_of(lo, hi):
            p = 0.5 * lo + 0.5 * hi
            p = jnp.where((lo < 0.0) & (hi > 0.0), 0.0, p)
            p = jnp.where((lo == 0.0) & (hi > TINY), TINY, p)
            return jnp.where((hi == 0.0) & (lo < -TINY), -TINY, p)

        def cond(st):
            lo, hi, c_lo, c_hi = st
            p = probe_of(lo, hi)
            done = (c_lo == TOPK) | (hi - lo <= TINY) | ~((p > lo) & (p < hi))
            return jnp.max(jnp.where(done, 0.0, 1.0)) > 0.0

        def step(st):
            lo, hi, c_lo, c_hi = st
            p = probe_of(lo, hi)
            c = count_ge(p)
            ge = c >= TOPK
            return (jnp.where(ge, p, lo), jnp.where(ge, hi, p),
                    jnp.where(ge, c, c_lo), jnp.where(ge, c_hi, c))

        n_causal = (tq[0:1, :] + 1).astype(F32)
        st0 = (smin, smax + (jnp.abs(smax) * 1e-6 + TINY), n_causal, jnp.zeros((1, LANES), F32))
        st1 = lax.fori_loop(0, FREE_STEPS, lambda _, st: step(st), st0)
        lo, hi, c_lo, c_hi = lax.while_loop(cond, lambda st: step(step(st)), st1)
        thr_ref[...] = jnp.broadcast_to(lo, thr_ref.shape)

        tied = c_lo > TOPK

        @pl.when(jnp.max(jnp.where(tied, 1.0, 0.0)) > 0.0)
        def _():
            keep = TOPK - c_hi
            tri = jnp.where(lax.broadcasted_iota(I32, (CK, CK), 0) >= lax.broadcasted_iota(I32, (CK, CK), 1),
                            1.0, 0.0).astype(BF16)

            def drop(c, before):
                x = sc_ref[c]
                eq = x == lo
                prefix = before + jnp.dot(tri, jnp.where(eq, 1.0, 0.0).astype(BF16),
                                          preferred_element_type=F32)
                sc_ref[c] = jnp.where(eq & tied & (prefix > keep), NEG, x)
                return prefix[CK - 1:CK, :]

            lax.fori_loop(0, nck, drop, jnp.zeros((1, LANES), F32))

    thr = thr_ref[0:1, :]
    rel_s = lax.broadcasted_iota(I32, (SUB, LANES), 0)
    tq_s = n * BLOCK + lax.broadcasted_iota(I32, (SUB, LANES), 1)
    last = pl.num_programs(1) - 1
    q2_t = bqt_ref[0, 0]

    def key_rows(t):
        return kb_ref[0, pl.ds(pl.multiple_of(jnp.minimum(t, last) * SUB, SUB), SUB), :]

    def value_rows(t):
        return vt_ref[0, jnp.clip(t, 0, last), 0:ACC_ROWS, :]

    def select_bias(t):
        tc = jnp.minimum(t, n)
        rows = pl.ds(pl.multiple_of((tc % 2) * SUB, SUB), SUB)
        sel = (sc_ref[tc // 2, rows, :] >= thr) & (t * SUB + rel_s <= tq_s)
        return jnp.where(sel, 0.0, NEG)

    qf = q2_t[0:HEAD_DIM].astype(F32)
    q_norm2 = jnp.sum(qf * qf, axis=0, keepdims=True)
    k_norm2 = jnp.concatenate([kn_ref[0, 0:1, :]] * B_HEADS, axis=1)
    tq_row = jnp.concatenate([tq_s[0:1].astype(F32)] * B_HEADS, axis=1)
    slope_row = jnp.concatenate([jnp.full((1, LANES), float(np.float32(sum(t))), F32) for t in SLOPE2_TERMS],
                                axis=1)
    offset = jnp.sqrt(q_norm2 * k_norm2) * (1.0 + 2.0 ** -6) + 2.0 ** -6 + slope_row * tq_row
    terms, rest = [], -offset
    for _ in range(3):
        terms.append(rest.astype(BF16).astype(F32))
        rest = rest - terms[-1]
    row_q = lax.broadcasted_iota(I32, (LANES, B_HEADS * LANES), 0)
    q2_off = q2_t.astype(F32)
    for r, term in enumerate(terms):
        q2_off = jnp.where(row_q == POS_LANE + 6 + r, term, q2_off)
    q2_off = q2_off.astype(BF16)

    def stage_logits(t, k):
        s_ref[k] = jnp.dot(key_rows(t), q2_off, preferred_element_type=F32)

    def probabilities(t, k):
        negb = select_bias(t)
        for h in range(B_HEADS):
            cols = slice(h * LANES, (h + 1) * LANES)
            p_ref[k, :, cols] = jnp.exp2(s_ref[k, :, cols] + negb).astype(BF16)

    def pv_accumulate(t, k):
        values = jnp.concatenate([value_rows(t), value_rows(t + 1)], axis=1)
        probs = jnp.concatenate([p_ref[k], p_ref[k + 1]], axis=0)
        acc_ref[...] += jnp.dot(values, probs, preferred_element_type=F32)

    def group(g, carry):
        for k in range(0, PIPE, 2):
            pv_accumulate((g - 1) * PIPE + k, k)
            for kk in (k, k + 1):
                probabilities(g * PIPE + kk, kk)
                stage_logits((g + 1) * PIPE + kk, kk)
        return carry

    ngroups = n // PIPE + 1
    acc_ref[...] = jnp.zeros_like(acc_ref)
    for k in range(PIPE):
        stage_logits(k, k)
    p_ref[...] = jnp.zeros(p_ref.shape, BF16)
    lax.fori_loop(0, ngroups, group, 0)
    for k in range(0, PIPE, 2):
        pv_accumulate((ngroups - 1) * PIPE + k, k)

    @pl.when(jnp.logical_not(jnp.min(acc_ref[HEAD_DIM:HEAD_DIM + 1, :]) > MIN_DENOMINATOR))
    def _():
        acc_ref[...] = jnp.zeros_like(acc_ref)

        def exact_step(t, m):
            s = jnp.dot(key_rows(t), q2_t, preferred_element_type=F32)
            negb = select_bias(t)
            ps, alphas, m_out = [], [], []
            for h in range(B_HEADS):
                l = s[:, h * LANES:(h + 1) * LANES] + negb
                m_new = jnp.maximum(m[h], jnp.max(l, axis=0, keepdims=True))
                alphas.append(jnp.exp2(m[h] - m_new))
                ps.append(jnp.exp2(l - m_new).astype(BF16))
                m_out.append(m_new)
            pv = jnp.dot(value_rows(t), jnp.concatenate(ps, axis=1), preferred_element_type=F32)
            acc_ref[...] = acc_ref[...] * jnp.concatenate(alphas, axis=1) + pv
            return m_out

        lax.fori_loop(0, n + 1, exact_step, [jnp.full((1, LANES), NEG, F32) for _ in range(B_HEADS)])

    outs = []
    pad = jnp.zeros((LANES - HEAD_DIM, LANES), F32)
    for h in range(B_HEADS):
        blk = acc_ref[:, h * LANES:(h + 1) * LANES]
        o_t = blk[0:HEAD_DIM] / blk[HEAD_DIM:HEAD_DIM + 1]
        outs.append(jnp.concatenate([o_t, pad], axis=0).T[:, 0:HEAD_DIM])
    yb_ref[0] = jnp.concatenate(outs, axis=1).astype(BF16)


def _knorm_kernel(kb_ref, o_ref):
    k = kb_ref[0].astype(F32)
    lane = lax.broadcasted_iota(I32, k.shape, 1)
    sq = jnp.where(lane < HEAD_DIM, k * k, 0.0)
    o_ref[0] = jnp.full(o_ref.shape[1:], jnp.max(jnp.sum(sq, axis=1, keepdims=True)), F32)


def _knorm(kb):
    bsz, t, _ = kb.shape
    return pl.pallas_call(
        _knorm_kernel,
        grid=(bsz,),
        in_specs=[pl.BlockSpec((1, t, LANES), lambda b: (b, 0, 0))],
        out_specs=pl.BlockSpec((1, 8, LANES), lambda b: (b, 0, 0)),
        out_shape=jax.ShapeDtypeStruct((bsz, 8, LANES), F32),
        compiler_params=_cparams("parallel"),
        name="knorm",
    )(kb)


def _dsa(iqt, iw, bqt, ki, kb, vt, kn):
    bsz, t, _ = iw.shape
    nb = t // BLOCK
    return pl.pallas_call(
        _dsa_kernel,
        grid=(bsz, nb),
        in_specs=[
            pl.BlockSpec((1, 1, LANES, IDX_HEADS * BLOCK), lambda b, n: (b, n, 0, 0)),
            pl.BlockSpec((1, BLOCK, LANES), lambda b, n: (b, n, 0)),
            pl.BlockSpec((1, 1, LANES, B_HEADS * BLOCK), lambda b, n: (b, n, 0, 0)),
            pl.BlockSpec((1, t, LANES), lambda b, n: (b, 0, 0)),
            pl.BlockSpec((1, t, LANES), lambda b, n: (b, 0, 0)),
            pl.BlockSpec((1, nb, LANES, BLOCK), lambda b, n: (b, 0, 0, 0)),
            pl.BlockSpec((1, 8, LANES), lambda b, n: (b, 0, 0)),
        ],
        out_specs=pl.BlockSpec((1, BLOCK, B_HEADS * HEAD_DIM), lambda b, n: (b, n, 0)),
        out_shape=jax.ShapeDtypeStruct((bsz, t, B_HEADS * HEAD_DIM), BF16),
        scratch_shapes=[
            pltpu.VMEM((t // CK, CK, LANES), F32),
            pltpu.VMEM((8, LANES), F32),
            pltpu.VMEM((PIPE, SUB, B_HEADS * LANES), F32),
            pltpu.VMEM((PIPE, SUB, B_HEADS * LANES), BF16),
            pltpu.VMEM((ACC_ROWS, B_HEADS * LANES), F32),
        ],
        compiler_params=_cparams("parallel", "arbitrary"),
        name="dsa",
    )(iqt, iw, bqt, ki, kb, vt, kn)


MERGE_TM = 1024


def _merge_kernel(h_ref, ya_ref, yb_ref, g_ref, wa_ref, wb_ref, wo_ref, o_ref):
    ta = jnp.dot(ya_ref[...], wa_ref[...], preferred_element_type=F32)
    tb = jnp.dot(yb_ref[...], wb_ref[...], preferred_element_type=F32)
    g = g_ref[...]
    mix = (g[:, :D_MODEL] * ta + g[:, D_MODEL:] * tb).astype(BF16)
    o_ref[...] = h_ref[...] + jnp.dot(mix, wo_ref[...], preferred_element_type=F32)


def _merge(h, ya, yb, g, wa, wb, wo):
    n = h.shape[0]
    row = lambda w: pl.BlockSpec((MERGE_TM, w), lambda i: (i, 0))
    full = lambda a: pl.BlockSpec(a.shape, lambda i: (0, 0))
    return pl.pallas_call(
        _merge_kernel,
        grid=(n // MERGE_TM,),
        in_specs=[row(D_MODEL), row(ya.shape[1]), row(yb.shape[1]), row(2 * D_MODEL),
                  full(wa), full(wb), full(wo)],
        out_specs=row(D_MODEL),
        out_shape=jax.ShapeDtypeStruct((n, D_MODEL), F32),
        compiler_params=_cparams("parallel"),
        name="merge",
    )(h, ya, yb, g, wa, wb, wo)


def _ple_kernel(h_ref, p_ref, lnp_ref, wg_ref, wp_ref, lnf_ref, o_ref):
    x = h_ref[...]
    u = _rms(x, lnp_ref[...]).astype(BF16)
    gate = jax.nn.sigmoid(jnp.dot(u, wg_ref[...], preferred_element_type=F32))
    proj = jnp.dot(p_ref[...].astype(BF16), wp_ref[...], preferred_element_type=F32)
    o_ref[...] = _rms(x + gate * proj, lnf_ref[...])


def _ple(h, p, lnp, wg, wp, lnf):
    n = h.shape[0]
    row = lambda w: pl.BlockSpec((MERGE_TM, w), lambda i: (i, 0))
    full = lambda a: pl.BlockSpec(a.shape, lambda i: (0, 0))
    return pl.pallas_call(
        _ple_kernel,
        grid=(n // MERGE_TM,),
        in_specs=[row(D_MODEL), row(D_PLE), full(lnp), full(wg), full(wp), full(lnf)],
        out_specs=row(D_MODEL),
        out_shape=jax.ShapeDtypeStruct((n, D_MODEL), F32),
        compiler_params=_cparams("parallel"),
        name="ple",
    )(h, p, lnp, wg, wp, lnf)


def kernel(x, p, ln_ffn1, w_ffn1_in, w_ffn1_out, ln_mix, w_in, a_sink, w_br_a, w_br_b, w_out,
           ln_ffn2, w_ffn2_in, w_ffn2_out, ln_ple, w_ple_gate, w_ple_proj, ln_final):
    bsz, t, d = x.shape
    assert p.shape[0] == 1, "the final norm is fused into the single layer's last kernel"
    n = bsz * t
    h = x.reshape(n, d)
    h = _ffn(h, ln_ffn1, w_ffn1_in[0].astype(BF16), w_ffn1_out[0].astype(BF16))
    aqt, ka, avt, iqt, bqt, ki, kb, vt, iw, g = _in_proj(h, ln_mix, _split_in_proj(w_in[0]), t // BLOCK)
    tok = lambda a: a.reshape(bsz, t, a.shape[-1])
    blk = lambda a: a.reshape(bsz, t // BLOCK, a.shape[-2], a.shape[-1])
    ya = _swa(a_sink[0], blk(aqt), tok(ka), blk(avt))
    yb = _dsa(blk(iqt), tok(iw), blk(bqt), tok(ki), tok(kb), blk(vt), _knorm(tok(kb)))
    h = _merge(h, ya.reshape(n, -1), yb.reshape(n, -1), g,
               w_br_a[0].astype(BF16), w_br_b[0].astype(BF16), w_out[0].astype(BF16))
    h = _ffn(h, ln_ffn2, w_ffn2_in[0].astype(BF16), w_ffn2_out[0].astype(BF16))
    h = _ple(h, p[0].reshape(n, -1), ln_ple, w_ple_gate[0].astype(BF16),
             w_ple_proj[0].astype(BF16), ln_final[None])
    return h.reshape(bsz, t, d)
```
